```python
import jax
import jax.numpy as jnp
from jax import lax
import numpy as np

D_MODEL = 1024
BATCH = 8
SEQ = 4096
DEPTH = 2

CTX_LEN = 256
GRID_W = 64
HEAD_DIM = D_MODEL // 16
N_FOURIER_GROUPS = 4
N_NA_HEADS = 4
N_GLA_HEADS = 4
N_GQA_HEADS = 4
N_GQA_KV_HEADS = 2
GQA_GROUP = N_GQA_HEADS // N_GQA_KV_HEADS
GLA_DK = HEAD_DIM // 2
GLA_DV = HEAD_DIM
GLA_GATE_RANK = 16
GLA_TAU = 16.0
GLA_CHUNK = 64
NA_ROWS = 8
NA_COLS = 16
Q_BLOCK = 128
D_FF = 256 * ((8 * D_MODEL // 3 + 255) // 256)
ROPE_THETA = 10000.0
EPS = 1e-6
N_MOD = 9

W_FOURIER = N_FOURIER_GROUPS * HEAD_DIM
W_NA = N_NA_HEADS * HEAD_DIM
W_GLA = N_GLA_HEADS * GLA_DV
W_GQA = N_GQA_HEADS * HEAD_DIM
D_MIX = W_FOURIER + W_NA + W_GLA + W_GQA

IN_LAYOUT = (
    ('fourier', W_FOURIER),
    ('na_q', W_NA), ('na_k', W_NA), ('na_v', W_NA),
    ('gla_q', N_GLA_HEADS * GLA_DK), ('gla_k', N_GLA_HEADS * GLA_DK),
    ('gla_v', W_GLA), ('gla_r', W_GLA),
    ('gla_gf', GLA_GATE_RANK), ('gla_gb', GLA_GATE_RANK),
    ('gqa_q', W_GQA), ('gqa_k', N_GQA_KV_HEADS * HEAD_DIM), ('gqa_v', N_GQA_KV_HEADS * HEAD_DIM),
)
D_IN = sum(w for _, w in IN_LAYOUT)

kernel_name = 'hybrid_parallel_group_diffusion_block'


def rms_norm(x, gain=None):
    xf = x.astype(jnp.float32)
    y = xf * lax.rsqrt(jnp.mean(xf * xf, axis=-1, keepdims=True) + EPS)
    if gain is not None:
        y = y * gain.astype(jnp.float32)
    return y.astype(x.dtype)


def modulate(h, shift, scale):
    return h * (1.0 + scale) + shift


def swiglu(h, w1, w3, w2):
    return (jax.nn.silu(h @ w1) * (h @ w3)) @ w2


def heads(t, n, d):
    return t.reshape(t.shape[:-1] + (n, d))


def split_in(h):
    offsets = np.cumsum([w for _, w in IN_LAYOUT])[:-1].tolist()
    parts = jnp.split(h, offsets, axis=-1)
    return {name: part for (name, _), part in zip(IN_LAYOUT, parts)}


def axial_rope(n_tokens):
    t = jnp.arange(n_tokens, dtype=jnp.int32)
    row = (t // GRID_W).astype(jnp.float32)
    col = (t % GRID_W).astype(jnp.float32)
    n_freq = HEAD_DIM // 4
    inv_freq = ROPE_THETA ** (-jnp.arange(n_freq, dtype=jnp.float32) / n_freq)
    ang = jnp.concatenate([row[:, None] * inv_freq, col[:, None] * inv_freq], axis=-1)
    return jnp.cos(ang)[:, None, :], jnp.sin(ang)[:, None, :]


def apply_rope(x, cos, sin):
    xf = x.astype(jnp.float32)
    x1, x2 = jnp.split(xf, 2, axis=-1)
    return jnp.concatenate([x1 * cos - x2 * sin, x1 * sin + x2 * cos], axis=-1).astype(x.dtype)


def fourier_mix(u):
    b_, l_, _ = u.shape
    uf = u.astype(jnp.float32).reshape(b_, l_, N_FOURIER_GROUPS, HEAD_DIM)
    y = jnp.fft.fft2(uf, axes=(1, 3), norm='ortho').real
    return y.reshape(b_, l_, W_FOURIER).astype(u.dtype)


def block_attention(q, k, v):
    b_, lq, hk, g, dh = q.shape
    nb = lq // Q_BLOCK
    qb = (q * dh ** -0.5).reshape(b_, nb, Q_BLOCK, hk, g, dh).swapaxes(0, 1)

    def one_block(q_blk):
        s = jnp.einsum('bqhgd,bkhd->bhgqk', q_blk, k).astype(jnp.float32)
        p = jax.nn.softmax(s, axis=-1).astype(v.dtype)
        return jnp.einsum('bhgqk,bkhd->bqhgd', p, v)

    o = lax.map(one_block, qb)
    return o.swapaxes(0, 1).reshape(b_, lq, hk * g * dh)


def neighborhood_attention(q, k, v, k_ctx, v_ctx, rpb):
    b_, s_, h_, dh = q.shape
    rows = s_ // GRID_W
    kr = min(NA_ROWS, rows)
    kg = k.reshape(b_, rows, GRID_W, h_, dh)
    vg = v.reshape(b_, rows, GRID_W, h_, dh)
    qg = q.reshape(b_, rows, GRID_W, h_, dh).transpose(1, 0, 2, 3, 4)
    r_pos = jnp.arange(rows)
    c_pos = jnp.arange(GRID_W)
    row_start = jnp.clip(r_pos - kr // 2, 0, rows - kr)
    rel_row = row_start[:, None] + jnp.arange(kr)[None, :] - r_pos[:, None]
    col_start = jnp.clip(c_pos - NA_COLS // 2, 0, GRID_W - NA_COLS)
    col_idx = col_start[:, None] + jnp.arange(NA_COLS)[None, :]
    rel_col = col_idx - c_pos[:, None]
    bias_col = rpb.astype(jnp.float32)[:, :, rel_col + NA_COLS - 1]
    scale = dh ** -0.5
    n_win = kr * NA_COLS

    def row_block(args):
        q_r, rs, rr = args
        k_blk = lax.dynamic_slice_in_dim(kg, rs, kr, axis=1)[:, :, col_idx]
        v_blk = lax.dynamic_slice_in_dim(vg, rs, kr, axis=1)[:, :, col_idx]
        bias = bias_col[:, rr + NA_ROWS - 1].transpose(0, 2, 1, 3)
        s_win = jnp.einsum('bchd,brckhd->bhcrk', q_r, k_blk).astype(jnp.float32) * scale + bias[None]
        s_ctx = jnp.einsum('bchd,bjhd->bhcj', q_r, k_ctx).astype(jnp.float32) * scale
        s = jnp.concatenate([s_win.reshape(b_, h_, GRID_W, n_win), s_ctx], axis=-1)
        p = jax.nn.softmax(s, axis=-1).astype(v.dtype)
        p_win = p[..., :n_win].reshape(b_, h_, GRID_W, kr, NA_COLS)
        p_ctx = p[..., n_win:]
        return (jnp.einsum('bhcrk,brckhd->bchd', p_win, v_blk)
                + jnp.einsum('bhcj,bjhd->bchd', p_ctx, v_ctx))

    out = lax.map(row_block, (qg, row_start, rel_row))
    return out.transpose(1, 0, 2, 3, 4).reshape(b_, s_, h_ * dh)


def gla_scan(q, k, v, log_a, s0):
    b_, h_, l_, _ = q.shape
    dv = v.shape[-1]
    n_chunks = l_ // GLA_CHUNK

    def chunks(t):
        return t.reshape(b_, h_, n_chunks, GLA_CHUNK, t.shape[-1]).transpose(2, 0, 1, 3, 4)

    lower_tri = jnp.tril(jnp.ones((GLA_CHUNK, GLA_CHUNK), dtype=bool))[None, None, :, :, None]

    def step(s, inp):
        qc, kc, vc, ac = inp
        qf, kf, vf = (t.astype(jnp.float32) for t in (qc, kc, vc))
        b = jnp.cumsum(ac.astype(jnp.float32), axis=2)
        rel = b[:, :, :, None, :] - b[:, :, None, :, :]
        decay = jnp.where(lower_tri, jnp.exp(jnp.minimum(rel, 0.0)), 0.0)
        scores = jnp.einsum('bhid,bhjd,bhijd->bhij', qf, kf, decay)
        o = (jnp.einsum('bhcd,bhde->bhce', qf * jnp.exp(b), s)
             + jnp.einsum('bhij,bhje->bhie', scores, vf))
        b_last = b[:, :, -1:, :]
        s_new = (jnp.exp(b_last[:, :, 0, :])[..., None] * s
                 + jnp.einsum('bhcd,bhce->bhde', kf * jnp.exp(b_last - b), vf))
        return s_new, o.astype(v.dtype)

    s_final, o = lax.scan(step, s0, (chunks(q), chunks(k), chunks(v), chunks(log_a)))
    return o.transpose(1, 2, 0, 3, 4).reshape(b_, h_, l_, dv), s_final


def bidir_gla(q, k, v, a_f, a_b, s0_f, s0_b):
    flip = lambda t: jnp.flip(t, axis=2)
    o_f, s_f = gla_scan(q, k, v, a_f, s0_f)
    o_b, s_b = gla_scan(flip(q), flip(k), flip(v), flip(a_b), s0_b)
    return o_f + flip(o_b), s_f, s_b


def gla_prepare(p, w_gf, b_gf, w_gb, b_gb):
    to_bhld = lambda t, d: heads(t, N_GLA_HEADS, d).transpose(0, 2, 1, 3)
    q = to_bhld(p['gla_q'] * GLA_DK ** -0.5, GLA_DK)
    k = to_bhld(p['gla_k'], GLA_DK)
    v = to_bhld(p['gla_v'], GLA_DV)
    a_f = to_bhld(jax.nn.log_sigmoid((p['gla_gf'] @ w_gf + b_gf).astype(jnp.float32)) / GLA_TAU, GLA_DK)
    a_b = to_bhld(jax.nn.log_sigmoid((p['gla_gb'] @ w_gb + b_gb).astype(jnp.float32)) / GLA_TAU, GLA_DK)
    return q, k, v, a_f, a_b


def gla_output(o, r, gain):
    b_, _, l_, _ = o.shape
    o = rms_norm(o.transpose(0, 2, 1, 3), gain).reshape(b_, l_, W_GLA)
    return o * jax.nn.silu(r)


def token_mixing(hx, hz, cos, sin, na_q_norm, na_k_norm, na_rpb, gla_w_gate_f, gla_b_gate_f,
                 gla_w_gate_b, gla_b_gate_b, gla_norm, gqa_q_norm, gqa_k_norm, with_ctx_out):
    px, pz = split_in(hx), split_in(hz)
    b_, s_ = hx.shape[0], hx.shape[1]
    lc = hz.shape[1]
    a_x = fourier_mix(px['fourier'])
    na_q = lambda p: rms_norm(heads(p['na_q'], N_NA_HEADS, HEAD_DIM), na_q_norm)
    na_k = lambda p: rms_norm(heads(p['na_k'], N_NA_HEADS, HEAD_DIM), na_k_norm)
    na_v = lambda p: heads(p['na_v'], N_NA_HEADS, HEAD_DIM)
    kz_na, vz_na = na_k(pz), na_v(pz)
    b_x = neighborhood_attention(na_q(px), na_k(px), na_v(px), kz_na, vz_na, na_rpb)
    gates = (gla_w_gate_f, gla_b_gate_f, gla_w_gate_b, gla_b_gate_b)
    s0 = jnp.zeros((b_, N_GLA_HEADS, GLA_DK, GLA_DV), jnp.float32)
    oz_gla, s_f, s_b = bidir_gla(*gla_prepare(pz, *gates), s0, s0)
    ox_gla, _, _ = bidir_gla(*gla_prepare(px, *gates), s_f, s_b)
    c_x = gla_output(ox_gla, px['gla_r'], gla_norm)
    gqa_q = lambda p: rms_norm(heads(p['gqa_q'], N_GQA_HEADS, HEAD_DIM), gqa_q_norm)
    gqa_k = lambda p: rms_norm(heads(p['gqa_k'], N_GQA_KV_HEADS, HEAD_DIM), gqa_k_norm)
    gqa_v = lambda p: heads(p['gqa_v'], N_GQA_KV_HEADS, HEAD_DIM)
    kz_gqa, vz_gqa = gqa_k(pz), gqa_v(pz)
    qx_gqa = apply_rope(gqa_q(px), cos, sin).reshape(b_, s_, N_GQA_KV_HEADS, GQA_GROUP, HEAD_DIM)
    kx_gqa = apply_rope(gqa_k(px), cos, sin)
    d_x = block_attention(qx_gqa, jnp.concatenate([kx_gqa, kz_gqa], axis=1),
                          jnp.concatenate([gqa_v(px), vz_gqa], axis=1))
    out_x = jnp.concatenate([a_x, b_x, c_x, d_x], axis=-1)
    if not with_ctx_out:
        return out_x, None
    a_z = fourier_mix(pz['fourier'])
    b_z = block_attention(na_q(pz)[:, :, :, None, :], kz_na, vz_na)
    c_z = gla_output(oz_gla, pz['gla_r'], gla_norm)
    d_z = block_attention(gqa_q(pz).reshape(b_, lc, N_GQA_KV_HEADS, GQA_GROUP, HEAD_DIM), kz_gqa, vz_gqa)
    return out_x, jnp.concatenate([a_z, b_z, c_z, d_z], axis=-1)


def setup_inputs(seed: int = 0) -> dict:
    key = jax.random.key(seed)
    ks = iter(jax.random.split(key, 32))
    f32 = jnp.float32

    def nrm(shape, std):
        return jax.random.normal(next(ks), shape, f32) * std

    L, D = DEPTH, D_MODEL
    return {
        'x': nrm((BATCH, SEQ, D), 1.0),
        'c': nrm((BATCH, D), 1.0),
        'ctx': nrm((BATCH, CTX_LEN, D), 1.0),
        'c_ctx': nrm((D,), 1.0),
        'w_mod': nrm((L, D, N_MOD * D), 0.5 * D ** -0.5),
        'b_mod': nrm((L, N_MOD * D), 0.02),
        'ffn1_w1': nrm((L, D, D_FF), D ** -0.5),
        'ffn1_w3': nrm((L, D, D_FF), D ** -0.5),
        'ffn1_w2': nrm((L, D_FF, D), D_FF ** -0.5),
        'w_in': nrm((L, D, D_IN), D ** -0.5),
        'na_q_norm': 1.0 + nrm((L, HEAD_DIM), 0.1),
        'na_k_norm': 1.0 + nrm((L, HEAD_DIM), 0.1),
        'na_rpb': nrm((L, N_NA_HEADS, 2 * NA_ROWS - 1, 2 * NA_COLS - 1), 0.1),
        'gla_w_gate_f': nrm((L, GLA_GATE_RANK, N_GLA_HEADS * GLA_DK), GLA_GATE_RANK ** -0.5),
        'gla_b_gate_f': nrm((L, N_GLA_HEADS * GLA_DK), 0.1),
        'gla_w_gate_b': nrm((L, GLA_GATE_RANK, N_GLA_HEADS * GLA_DK), GLA_GATE_RANK ** -0.5),
        'gla_b_gate_b': nrm((L, N_GLA_HEADS * GLA_DK), 0.1),
        'gla_norm': 1.0 + nrm((L, GLA_DV), 0.1),
        'gqa_q_norm': 1.0 + nrm((L, HEAD_DIM), 0.1),
        'gqa_k_norm': 1.0 + nrm((L, HEAD_DIM), 0.1),
        'w_out': nrm((L, D_MIX, D), D_MIX ** -0.5),
        'ffn2_w1': nrm((L, D, D_FF), D ** -0.5),
        'ffn2_w3': nrm((L, D, D_FF), D ** -0.5),
        'ffn2_w2': nrm((L, D_FF, D), D_FF ** -0.5),
    }


def reference(x, c, ctx, c_ctx, w_mod, b_mod, ffn1_w1, ffn1_w3, ffn1_w2, w_in,
              na_q_norm, na_k_norm, na_rpb, gla_w_gate_f, gla_b_gate_f, gla_w_gate_b, gla_b_gate_b,
              gla_norm, gqa_q_norm, gqa_k_norm, w_out, ffn2_w1, ffn2_w3, ffn2_w2):
    cos, sin = axial_rope(x.shape[1])
    z = ctx
    for l in range(DEPTH):
        last = l == DEPTH - 1
        mx = jnp.split((jax.nn.silu(c) @ w_mod[l] + b_mod[l])[:, None, :], N_MOD, axis=-1)
        mz = jnp.split(jax.nn.silu(c_ctx) @ w_mod[l] + b_mod[l], N_MOD, axis=-1)
        x = x + 0.5 * mx[2] * swiglu(modulate(rms_norm(x), mx[0], mx[1]), ffn1_w1[l], ffn1_w3[l], ffn1_w2[l])
        z = z + 0.5 * mz[2] * swiglu(modulate(rms_norm(z), mz[0], mz[1]), ffn1_w1[l], ffn1_w3[l], ffn1_w2[l])
        hx = modulate(rms_norm(x), mx[3], mx[4]) @ w_in[l]
        hz = modulate(rms_norm(z), mz[3], mz[4]) @ w_in[l]
        mix_x, mix_z = token_mixing(hx, hz, cos, sin, na_q_norm[l], na_k_norm[l], na_rpb[l],
                                    gla_w_gate_f[l], gla_b_gate_f[l], gla_w_gate_b[l], gla_b_gate_b[l],
                                    gla_norm[l], gqa_q_norm[l], gqa_k_norm[l], not last)
        x = x + mx[5] * (mix_x @ w_out[l])
        x = x + 0.5 * mx[8] * swiglu(modulate(rms_norm(x), mx[6], mx[7]), ffn2_w1[l], ffn2_w3[l], ffn2_w2[l])
        if not last:
            z = z + mz[5] * (mix_z @ w_out[l])
            z = z + 0.5 * mz[8] * swiglu(modulate(rms_norm(z), mz[6], mz[7]), ffn2_w1[l], ffn2_w3[l], ffn2_w2[l])
    return x
```

```python
import functools
import math

import jax
import jax.numpy as jnp
from jax import lax
from jax.experimental import pallas as pl
from jax.experimental.pallas import tpu as pltpu

F32 = jnp.float32
BF16 = jnp.bfloat16

HEAD_DIM = 64
GRID_W = 64
NA_ROWS = 8
NA_COLS = 16
GLA_DK = 32
GLA_TAU = 16.0
GLA_CHUNK = 64
GLA_LEVELS = (32, 16, 8, 4, 2, 1)
ROPE_THETA = 10000.0
EPS = 1e-6
N_MOD = 9
NEG = -1e30

VMEM_LIMIT = 56 * 1024 * 1024


def _cparams(*sem):
    return pltpu.CompilerParams(dimension_semantics=sem, vmem_limit_bytes=VMEM_LIMIT)


def _dot(a, b):
    return jnp.dot(a, b, preferred_element_type=F32)


def _dot_nt(a, b):
    return lax.dot_general(a, b, (((1,), (1,)), ((), ())), preferred_element_type=F32)


def _dot_tn(a, b):
    return lax.dot_general(a, b, (((0,), (0,)), ((), ())), preferred_element_type=F32)


def _sigmoid(x):
    return 1.0 / (1.0 + jnp.exp(-x))


def _silu(x):
    return x * _sigmoid(x)


def _log_sigmoid(x):
    return jnp.minimum(x, 0.0) - jnp.log(1.0 + jnp.exp(-jnp.abs(x)))


def _const_spec(shape):
    nd = len(shape)
    return pl.BlockSpec(shape, lambda *_: (0,) * nd)


def _mod_kernel(c_ref, w_ref, b_ref, o_ref):
    h = _silu(c_ref[...]).astype(BF16)
    o_ref[0] = _dot(h, w_ref[0].astype(BF16)) + b_ref[0]


def _modulation(cc, w_mod, b_mod):
    depth, d, n = w_mod.shape
    rows = cc.shape[0]
    tn = 1024
    return pl.pallas_call(
        _mod_kernel,
        out_shape=jax.ShapeDtypeStruct((depth, rows, n), F32),
        grid=(depth, n // tn),
        in_specs=[
            pl.BlockSpec((rows, d), lambda l, j: (0, 0)),
            pl.BlockSpec((1, d, tn), lambda l, j: (l, 0, j)),
            pl.BlockSpec((1, 1, tn), lambda l, j: (l, 0, j)),
        ],
        out_specs=pl.BlockSpec((1, rows, tn), lambda l, j: (l, 0, j)),
        compiler_params=_cparams("parallel", "parallel"),
        name="adaln_mod",
    )(cc, w_mod, b_mod.reshape(depth, 1, n))


def _norm_mod(x, shift, scale):
    h = x * lax.rsqrt(jnp.mean(x * x, axis=-1, keepdims=True) + EPS)
    return h * (1.0 + scale) + shift


def _swiglu_acc(hb, w1_ref, w3_ref, w2_ref, fc):
    d_ff = w1_ref.shape[1]
    acc = jnp.zeros((hb.shape[0], w2_ref.shape[1]), F32)
    for f0 in range(0, d_ff, fc):
        a = _dot(hb, w1_ref[:, f0:f0 + fc])
        b = _dot(hb, w3_ref[:, f0:f0 + fc])
        g = (_silu(a) * b).astype(BF16)
        acc = acc + _dot(g, w2_ref[f0:f0 + fc, :])
    return acc


def _ffn_kernel(x_ref, mod_ref, w1_ref, w3_ref, w2_ref, o_ref, *, row0, fc):
    x = x_ref[0]
    shift = mod_ref[0, row0:row0 + 1, :]
    scale = mod_ref[0, row0 + 1:row0 + 2, :]
    gate = mod_ref[0, row0 + 2:row0 + 3, :]
    hb = _norm_mod(x, shift, scale).astype(BF16)
    acc = _swiglu_acc(hb, w1_ref, w3_ref, w2_ref, fc)
    o_ref[0] = x + (0.5 * gate) * acc


def _ffn(x, mod, w1, w3, w2, row0, tm):
    nb, r, d = x.shape
    d_ff = w1.shape[1]
    return pl.pallas_call(
        functools.partial(_ffn_kernel, row0=row0, fc=256),
        out_shape=jax.ShapeDtypeStruct(x.shape, F32),
        grid=(nb, r // tm),
        in_specs=[
            pl.BlockSpec((1, tm, d), lambda b, t: (b, t, 0)),
            pl.BlockSpec((1, N_MOD, d), lambda b, t: (b, 0, 0)),
            _const_spec((d, d_ff)),
            _const_spec((d, d_ff)),
            _const_spec((d_ff, d)),
        ],
        out_specs=pl.BlockSpec((1, tm, d), lambda b, t: (b, t, 0)),
        compiler_params=_cparams("parallel", "parallel"),
        name="ffn_halfstep",
    )(x, mod, w1, w3, w2)


def _group_sum(xsq, gmat):
    hi = xsq.astype(BF16)
    lo = (xsq - hi.astype(F32)).astype(BF16)
    return _dot(hi, gmat) + _dot(lo, gmat)


def _head_rms(x, gmat, gain):
    ms = _group_sum(x * x, gmat) * (1.0 / HEAD_DIM)
    return x * lax.rsqrt(ms + EPS) * gain


def _rope128(x, cos, sin_signed):
    lane = lax.broadcasted_iota(jnp.int32, x.shape, 1)
    first_half = (lane % HEAD_DIM) < (HEAD_DIM // 2)
    partner = jnp.where(first_half, pltpu.roll(x, 96, 1), pltpu.roll(x, 32, 1))
    return x * cos + partner * sin_signed


def _inproj_kernel(x_ref, mod_ref, w_ref, g256_ref, g128_ref, cc_ref, sc_ref,
                   naq_g_ref, nak_g_ref, gq_g_ref, gk_g_ref, wg_ref, bg_ref, cos_ref, sin_ref,
                   uc_ref, us_ref, naq_ref, nak_ref, nav_ref, qka_ref, gv_ref, gr_ref,
                   q_ref, k_ref, v_ref, *, rope):
    x = x_ref[0]
    shift = mod_ref[0, 3:4, :]
    scale = mod_ref[0, 4:5, :]
    hb = _norm_mod(x, shift, scale).astype(BF16)
    h = _dot(hb, w_ref[...])
    g256 = g256_ref[...]
    g128 = g128_ref[...]
    ub = h[:, 0:256].astype(BF16)
    uc_ref[...] = _dot(ub, cc_ref[...]).astype(BF16)
    us_ref[...] = _dot(ub, sc_ref[...]).astype(BF16)
    naq_ref[0] = (_head_rms(h[:, 256:512], g256, naq_g_ref[...]) * HEAD_DIM ** -0.5).astype(BF16)
    nak_ref[0] = _head_rms(h[:, 512:768], g256, nak_g_ref[...]).astype(BF16)
    nav_ref[0] = h[:, 768:1024].astype(BF16)
    gates = _dot(h[:, 2304:2432].astype(BF16), wg_ref[...]) + bg_ref[...]
    log_a = _log_sigmoid(gates) / GLA_TAU
    qka_ref[0, :, 0:128] = h[:, 1024:1152] * GLA_DK ** -0.5
    qka_ref[0, :, 128:256] = h[:, 1152:1280]
    qka_ref[0, :, 256:512] = log_a
    gv_ref[0] = h[:, 1280:1536].astype(BF16)
    gr_ref[0] = h[:, 1536:1792]
    q = _head_rms(h[:, 1792:2048], g256, gq_g_ref[...])
    k = _head_rms(h[:, 2048:2176], g128, gk_g_ref[:, 0:128])
    if rope:
        cos = cos_ref[...]
        sin = sin_ref[...]
        q = jnp.concatenate([_rope128(q[:, 0:128], cos, sin), _rope128(q[:, 128:256], cos, sin)], axis=1)
        k = _rope128(k, cos, sin)
    q_ref[0] = (q * HEAD_DIM ** -0.5).astype(BF16)
    k_ref[0] = k.astype(BF16)
    v_ref[0] = h[:, 2176:2304].astype(BF16)


def _inproj(x, mod, lw, consts, cos_t, sin_t, rope, tm, four_index):
    nb, r, d = x.shape
    n_four = four_index[1]

    def act(width, dtype):
        return jax.ShapeDtypeStruct((nb, r, width), dtype)

    def act_spec(width):
        return pl.BlockSpec((1, tm, width), lambda b, t: (b, t, 0))

    four_shape = jax.ShapeDtypeStruct(n_four, BF16)
    four_spec = pl.BlockSpec((tm, 256), four_index[0])
    vec256 = _const_spec((1, 256))
    out_shapes = [four_shape, four_shape, act(256, BF16), act(256, BF16), act(256, BF16),
                  act(512, F32), act(256, BF16), act(256, F32),
                  act(256, BF16), act(128, BF16), act(128, BF16)]
    out_specs = [four_spec, four_spec, act_spec(256), act_spec(256), act_spec(256),
                 act_spec(512), act_spec(256), act_spec(256),
                 act_spec(256), act_spec(128), act_spec(128)]
    return pl.pallas_call(
        functools.partial(_inproj_kernel, rope=rope),
        out_shape=out_shapes,
        grid=(nb, r // tm),
        in_specs=[
            pl.BlockSpec((1, tm, d), lambda b, t: (b, t, 0)),
            pl.BlockSpec((1, N_MOD, d), lambda b, t: (b, 0, 0)),
            _const_spec(lw["w_in"].shape),
            _const_spec((256, 256)), _const_spec((128, 128)),
            _const_spec((256, 256)), _const_spec((256, 256)),
            vec256, vec256, vec256, vec256,
            _const_spec((128, 256)), vec256,
            pl.BlockSpec((tm, 128), lambda b, t: (t, 0)),
            pl.BlockSpec((tm, 128), lambda b, t: (t, 0)),
        ],
        out_specs=out_specs,
        compiler_params=_cparams("parallel", "parallel"),
        name="in_proj",
    )(x, mod, lw["w_in"], consts["g256"], consts["g128"], consts["cc"], consts["sc"],
      lw["naq_g"], lw["nak_g"], lw["gq_g"], lw["gk_g"], lw["wg"], lw["bg"], cos_t, sin_t)


def _dft_kernel(c_ref, s_ref, uc_ref, us_ref, o_ref, acc_ref):
    k = pl.program_id(1)

    @pl.when(k == 0)
    def _():
        acc_ref[...] = jnp.zeros_like(acc_ref)

    acc_ref[...] += _dot(c_ref[...], uc_ref[...]) - _dot(s_ref[...], us_ref[...])

    @pl.when(k == pl.num_programs(1) - 1)
    def _():
        o_ref[...] = acc_ref[...].astype(o_ref.dtype)


def _dft(c_tab, s_tab, uc, us, tm, tk):
    l, n = uc.shape
    return pl.pallas_call(
        _dft_kernel,
        out_shape=jax.ShapeDtypeStruct((l, n), BF16),
        grid=(l // tm, l // tk),
        in_specs=[
            pl.BlockSpec((tm, tk), lambda i, k: (i, k)),
            pl.BlockSpec((tm, tk), lambda i, k: (i, k)),
            pl.BlockSpec((tk, n), lambda i, k: (k, 0)),
            pl.BlockSpec((tk, n), lambda i, k: (k, 0)),
        ],
        out_specs=pl.BlockSpec((tm, n), lambda i, k: (i, 0)),
        scratch_shapes=[pltpu.VMEM((tm, n), F32)],
        compiler_params=_cparams("parallel", "arbitrary"),
        name="position_dft",
    )(c_tab, s_tab, uc, us)


def _stack_heads(q, n_heads, width):
    lane = lax.broadcasted_iota(jnp.int32, q.shape, 1)
    zero = jnp.zeros_like(q)
    return jnp.concatenate([jnp.where(lane // width == h, q, zero) for h in range(n_heads)], axis=0)


def _unstack_heads(o, n_heads, width):
    m = o.shape[0] // n_heads
    lane = lax.broadcasted_iota(jnp.int32, (m, o.shape[1]), 1)
    out = jnp.zeros((m, o.shape[1]), F32)
    for h in range(n_heads):
        out = out + jnp.where(lane // width == h, o[h * m:(h + 1) * m, :], 0.0)
    return out


def _softmax_pv(score_blocks, value_blocks):
    m = score_blocks[0].max(axis=-1, keepdims=True)
    for s in score_blocks[1:]:
        m = jnp.maximum(m, s.max(axis=-1, keepdims=True))
    l = None
    o = None
    for s, v in zip(score_blocks, value_blocks):
        p = jnp.exp(s - m)
        ps = p.sum(axis=-1, keepdims=True)
        pv = _dot(p.astype(BF16), v)
        l = ps if l is None else l + ps
        o = pv if o is None else o + pv
    return o * (1.0 / l)


def _na_kernel(q_ref, k_ref, v_ref, kc_ref, vc_ref, tb_ref, o_ref, *, rows_per_step, n_rows):
    rb = pl.program_id(1)
    kc = kc_ref[0]
    vc = vc_ref[0]
    for i in range(rows_per_step):
        r = rb * rows_per_step + i
        rs = jnp.clip(r - NA_ROWS // 2, 0, n_rows - NA_ROWS)
        off = r - rs
        q = q_ref[0, i * GRID_W:(i + 1) * GRID_W, :]
        qs = _stack_heads(q, 4, HEAD_DIM)
        start = pl.multiple_of(rs * GRID_W, GRID_W)
        kw = k_ref[0, pl.ds(start, NA_ROWS * GRID_W), :]
        vw = v_ref[0, pl.ds(start, NA_ROWS * GRID_W), :]
        s_w = _dot_nt(qs, kw) + tb_ref[off]
        s_c = _dot_nt(qs, kc)
        o = _softmax_pv([s_w, s_c], [vw, vc])
        o_ref[0, i * GRID_W:(i + 1) * GRID_W, :] = _unstack_heads(o, 4, HEAD_DIM).astype(BF16)


def _na(q, k, v, kc, vc, tb):
    nb, s, w = q.shape
    n_rows = s // GRID_W
    rps = 8
    lc = kc.shape[1]
    return pl.pallas_call(
        functools.partial(_na_kernel, rows_per_step=rps, n_rows=n_rows),
        out_shape=jax.ShapeDtypeStruct((nb, s, w), BF16),
        grid=(nb, n_rows // rps),
        in_specs=[
            pl.BlockSpec((1, rps * GRID_W, w), lambda b, t: (b, t, 0)),
            pl.BlockSpec((1, s, w), lambda b, t: (b, 0, 0)),
            pl.BlockSpec((1, s, w), lambda b, t: (b, 0, 0)),
            pl.BlockSpec((1, lc, w), lambda b, t: (b, 0, 0)),
            pl.BlockSpec((1, lc, w), lambda b, t: (b, 0, 0)),
            _const_spec(tb.shape),
        ],
        out_specs=pl.BlockSpec((1, rps * GRID_W, w), lambda b, t: (b, t, 0)),
        compiler_params=_cparams("parallel", "parallel"),
        name="neighborhood_attn",
    )(q, k, v, kc, vc, tb)


def _gqa_scores_lhs(q):
    return jnp.concatenate([_stack_heads(q[:, 0:128], 2, HEAD_DIM),
                            _stack_heads(q[:, 128:256], 2, HEAD_DIM)], axis=0)


def _gqa_merge(o):
    m = o.shape[0] // 4
    return jnp.concatenate([_unstack_heads(o[0:2 * m], 2, HEAD_DIM),
                            _unstack_heads(o[2 * m:4 * m], 2, HEAD_DIM)], axis=1)


def _gqa_kernel(q_ref, k_ref, v_ref, kc_ref, vc_ref, o_ref):
    qs = _gqa_scores_lhs(q_ref[0])
    s_x = _dot_nt(qs, k_ref[0])
    s_c = _dot_nt(qs, kc_ref[0])
    o = _softmax_pv([s_x, s_c], [v_ref[0], vc_ref[0]])
    o_ref[0] = _gqa_merge(o).astype(BF16)


def _gqa(q, k, v, kc, vc, tq):
    nb, s, _ = q.shape
    lc = kc.shape[1]
    return pl.pallas_call(
        _gqa_kernel,
        out_shape=jax.ShapeDtypeStruct((nb, s, 256), BF16),
        grid=(nb, s // tq),
        in_specs=[
            pl.BlockSpec((1, tq, 256), lambda b, t: (b, t, 0)),
            pl.BlockSpec((1, s, 128), lambda b, t: (b, 0, 0)),
            pl.BlockSpec((1, s, 128), lambda b, t: (b, 0, 0)),
            pl.BlockSpec((1, lc, 128), lambda b, t: (b, 0, 0)),
            pl.BlockSpec((1, lc, 128), lambda b, t: (b, 0, 0)),
        ],
        out_specs=pl.BlockSpec((1, tq, 256), lambda b, t: (b, t, 0)),
        compiler_params=_cparams("parallel", "parallel"),
        name="gqa_attn",
    )(q, k, v, kc, vc)


def _ctx_attn_kernel(naq_ref, nak_ref, nav_ref, q_ref, k_ref, v_ref, bz_ref, dz_ref):
    qs = _stack_heads(naq_ref[0], 4, HEAD_DIM)
    o = _softmax_pv([_dot_nt(qs, nak_ref[0])], [nav_ref[0]])
    bz_ref[0] = _unstack_heads(o, 4, HEAD_DIM).astype(BF16)
    qg = _gqa_scores_lhs(q_ref[0])
    og = _softmax_pv([_dot_nt(qg, k_ref[0])], [v_ref[0]])
    dz_ref[0] = _gqa_merge(og).astype(BF16)


def _ctx_attn(naq, nak, nav, q, k, v):
    nb, lc, _ = naq.shape

    def spec(w):
        return pl.BlockSpec((1, lc, w), lambda b: (b, 0, 0))

    out = jax.ShapeDtypeStruct((nb, lc, 256), BF16)
    return pl.pallas_call(
        _ctx_attn_kernel,
        out_shape=[out, out],
        grid=(nb,),
        in_specs=[spec(256), spec(256), spec(256), spec(256), spec(128), spec(128)],
        out_specs=[spec(256), spec(256)],
        compiler_params=_cparams("parallel"),
        name="context_attn",
    )(naq, nak, nav, q, k, v)


def _gla_kernel(qf_ref, vf_ref, qb_ref, vb_ref, z_ref, mask_ref, s0_ref, of_ref, ob_ref, sfin_ref,
                st_ref, *, chunks):
    p = pl.program_id(1)
    c = GLA_CHUNK

    @pl.when(p == 0)
    def _():
        st_ref[...] = s0_ref[0]

    row = lax.broadcasted_iota(jnp.int32, (256, 128), 0)
    lane = lax.broadcasted_iota(jnp.int32, (256, 128), 1)
    bd_mask = (row // HEAD_DIM) == (lane // GLA_DK)

    def one(d, in_ref, v_ref, o_ref, r0, a0):
        q = in_ref[0, r0:r0 + c, 0:128]
        k = in_ref[0, r0:r0 + c, 128:256]
        a = in_ref[0, r0:r0 + c, a0:a0 + 128]
        v = v_ref[0, r0:r0 + c, :]
        a_hi = a.astype(BF16)
        r1 = a - a_hi.astype(F32)
        a_mid = r1.astype(BF16)
        a_lo = (r1 - a_mid.astype(F32)).astype(BF16)
        e3 = _dot(z_ref[d], jnp.concatenate([a_hi, a_mid, a_lo], axis=1))
        e = e3[:, 0:128] + e3[:, 128:256] + e3[:, 256:384]
        x = jnp.exp(e)
        qe = (q * x[0:c]).astype(BF16)
        ke = (k * x[c:2 * c]).astype(BF16)
        decay = x[8 * c:8 * c + 1, :]
        st = st_ref[d]
        o = _dot_nt(qe, st.astype(BF16))
        row4 = lax.broadcasted_iota(jnp.int32, (4 * c, 128), 0)
        lane4 = lax.broadcasted_iota(jnp.int32, (4 * c, 128), 1)
        head_mask = (row4 // c) == (lane4 // GLA_DK)
        scores = jnp.zeros((4 * c, c), F32)
        n_lev = len(GLA_LEVELS)
        for lev in range(n_lev + 1):
            if lev < n_lev:
                xl = x[(2 + lev) * c:(3 + lev) * c]
                ql = q * xl
                kl = (k * xl).astype(BF16)
            else:
                ql = q
                kl = k.astype(BF16)
            qs = jnp.where(head_mask, jnp.concatenate([ql, ql, ql, ql], axis=0), 0.0).astype(BF16)
            scores = scores + _dot_nt(qs, kl) * mask_ref[d, lev]
        o_all = _dot(scores.astype(BF16), v)
        o_ref[0, r0:r0 + c, :] = o + _unstack_heads(o_all, 4, HEAD_DIM)
        upd = _dot_tn(v, ke)
        st_ref[d] = st * decay + jnp.where(bd_mask, upd, 0.0)

    for ci in range(chunks):
        one(0, qf_ref, vf_ref, of_ref, ci * c, 256)
        one(1, qb_ref, vb_ref, ob_ref, (chunks - 1 - ci) * c, 384)

    @pl.when(p == pl.num_programs(1) - 1)
    def _():
        sfin_ref[0] = st_ref[...]


def _gla(qka, v, s0, z_tab, masks):
    nb, r, _ = qka.shape
    chunks = 4
    blk = chunks * GLA_CHUNK
    nblk = r // blk
    out = jax.ShapeDtypeStruct((nb, r, 256), F32)
    return pl.pallas_call(
        functools.partial(_gla_kernel, chunks=chunks),
        out_shape=[out, out, jax.ShapeDtypeStruct((nb, 2, 256, 128), F32)],
        grid=(nb, nblk),
        in_specs=[
            pl.BlockSpec((1, blk, 512), lambda b, p: (b, p, 0)),
            pl.BlockSpec((1, blk, 256), lambda b, p: (b, p, 0)),
            pl.BlockSpec((1, blk, 512), lambda b, p: (b, nblk - 1 - p, 0)),
            pl.BlockSpec((1, blk, 256), lambda b, p: (b, nblk - 1 - p, 0)),
            _const_spec(z_tab.shape),
            _const_spec(masks.shape),
            pl.BlockSpec((1, 2, 256, 128), lambda b, p: (b, 0, 0, 0)),
        ],
        out_specs=[
            pl.BlockSpec((1, blk, 256), lambda b, p: (b, p, 0)),
            pl.BlockSpec((1, blk, 256), lambda b, p: (b, nblk - 1 - p, 0)),
            pl.BlockSpec((1, 2, 256, 128), lambda b, p: (b, 0, 0, 0)),
        ],
        scratch_shapes=[pltpu.VMEM((2, 256, 128), F32)],
        compiler_params=_cparams("parallel", "arbitrary"),
        name="gla_scan",
    )(qka, v, qka, v, z_tab, masks, s0)


def _outproj_kernel(x_ref, mod_ref, a_ref, b_ref, of_ref, ob_ref, r_ref, d_ref, w_ref, g256_ref,
                    gn_ref, o_ref):
    o = of_ref[0] + ob_ref[0]
    cx = (_head_rms(o, g256_ref[...], gn_ref[...]) * _silu(r_ref[0])).astype(BF16)
    acc = _dot(a_ref[...], w_ref[0:256, :])
    acc = acc + _dot(b_ref[0], w_ref[256:512, :])
    acc = acc + _dot(cx, w_ref[512:768, :])
    acc = acc + _dot(d_ref[0], w_ref[768:1024, :])
    o_ref[0] = x_ref[0] + mod_ref[0, 5:6, :] * acc


def _outproj(x, mod, a, a_index, b, o_f, o_b, r, dd, w_out, g256, gn, tm):
    nb, rows, d = x.shape

    def act_spec(width):
        return pl.BlockSpec((1, tm, width), lambda bb, t: (bb, t, 0))

    return pl.pallas_call(
        _outproj_kernel,
        out_shape=jax.ShapeDtypeStruct(x.shape, F32),
        grid=(nb, rows // tm),
        in_specs=[
            act_spec(d),
            pl.BlockSpec((1, N_MOD, d), lambda bb, t: (bb, 0, 0)),
            pl.BlockSpec((tm, 256), a_index),
            act_spec(256), act_spec(256), act_spec(256), act_spec(256), act_spec(256),
            _const_spec(w_out.shape), _const_spec((256, 256)), _const_spec((1, 256)),
        ],
        out_specs=act_spec(d),
        compiler_params=_cparams("parallel", "parallel"),
        name="out_proj",
    )(x, mod, a, b, o_f, o_b, r, dd, w_out, g256, gn)


def _dft_tables(l):
    n = jnp.arange(l, dtype=jnp.int32)
    ang = ((n[:, None] * n[None, :]) % l).astype(F32) * (2.0 * math.pi / l)
    scale = l ** -0.5
    return (jnp.cos(ang) * scale).astype(BF16), (jnp.sin(ang) * scale).astype(BF16)


def _channel_dft_tables():
    n = jnp.arange(256, dtype=jnp.int32)
    same = (n[:, None] // HEAD_DIM) == (n[None, :] // HEAD_DIM)
    ang = (((n[:, None] % HEAD_DIM) * (n[None, :] % HEAD_DIM)) % HEAD_DIM).astype(F32) * (2.0 * math.pi / HEAD_DIM)
    scale = HEAD_DIM ** -0.5
    cc = jnp.where(same, jnp.cos(ang) * scale, 0.0).astype(BF16)
    sc = jnp.where(same, jnp.sin(ang) * scale, 0.0).astype(BF16)
    return cc, sc


def _group_matrix(width):
    n = jnp.arange(width, dtype=jnp.int32)
    return ((n[:, None] // HEAD_DIM) == (n[None, :] // HEAD_DIM)).astype(BF16)


def _rope_tables(n_tokens):
    t = jnp.arange(n_tokens, dtype=jnp.int32)
    row = (t // GRID_W).astype(F32)
    col = (t % GRID_W).astype(F32)
    n_freq = HEAD_DIM // 4
    inv_freq = ROPE_THETA ** (-jnp.arange(n_freq, dtype=F32) / n_freq)
    ang = jnp.concatenate([row[:, None] * inv_freq, col[:, None] * inv_freq], axis=-1)
    cos, sin = jnp.cos(ang), jnp.sin(ang)
    cos_t = jnp.concatenate([cos, cos, cos, cos], axis=-1)
    sin_t = jnp.concatenate([-sin, sin, -sin, sin], axis=-1)
    return cos_t, sin_t


def _gla_tables():
    c = GLA_CHUNK
    i = jnp.arange(c, dtype=jnp.int32)[:, None]
    t = jnp.arange(c, dtype=jnp.int32)[None, :]
    ones_row = jnp.ones((16, c), jnp.bool_)

    def tables(forward):
        z = [t <= i, t > i] if forward else [t >= i, t < i]
        m = []
        for s in GLA_LEVELS:
            pos = i % (2 * s)
            base = i - pos
            bound = base + s - 1
            right = pos >= s
            if forward:
                zq = right & (t > bound) & (t <= i)
                zk = (~right) & (t > i) & (t <= bound)
            else:
                zq = (~right) & (t >= i) & (t <= bound)
                zk = right & (t > bound) & (t < i)
            z.append(zq | zk)
            same_parent = (i // (2 * s)) == (t // (2 * s))
            t_right = (t % (2 * s)) >= s
            if forward:
                m.append(same_parent & right & (~t_right))
            else:
                m.append(same_parent & (~right) & t_right)
        m.append(i == t)
        z.append(ones_row)
        zt = jnp.concatenate(z, axis=0).astype(BF16)
        mt = jnp.stack([jnp.concatenate([mm, mm, mm, mm], axis=0) for mm in m]).astype(F32)
        return zt, mt

    zf, mf = tables(True)
    zb, mb = tables(False)
    return jnp.stack([zf, zb]), jnp.stack([mf, mb])


def _na_bias_table(rpb):
    off = jnp.arange(NA_ROWS)[:, None, None, None, None]
    h = jnp.arange(4)[None, :, None, None, None]
    c = jnp.arange(GRID_W)[None, None, :, None, None]
    j = jnp.arange(NA_ROWS)[None, None, None, :, None]
    kc = jnp.arange(GRID_W)[None, None, None, None, :]
    cs = jnp.clip(c - NA_COLS // 2, 0, GRID_W - NA_COLS)
    valid = (kc >= cs) & (kc < cs + NA_COLS)
    ri = j - off + NA_ROWS - 1
    ci = jnp.clip(kc - c + NA_COLS - 1, 0, 2 * NA_COLS - 2)
    tb = jnp.where(valid, rpb.astype(F32)[h, ri, ci], NEG)
    return tb.reshape(NA_ROWS, 4 * GRID_W, NA_ROWS * GRID_W)


_GQA_PERM = (0, 2, 1, 3)


def _layer_weights(l, w_in, na_q_norm, na_k_norm, gla_w_gate_f, gla_b_gate_f, gla_w_gate_b, gla_b_gate_b,
                   gla_norm, gqa_q_norm, gqa_k_norm, w_out):
    wi = w_in[l]
    d = wi.shape[0]
    gq = wi[:, 1824:2080].reshape(d, 4, HEAD_DIM)[:, _GQA_PERM, :].reshape(d, 256)
    w_in_r = jnp.concatenate([wi[:, 0:1792], gq, wi[:, 2080:2336], wi[:, 1792:1824],
                              jnp.zeros((d, 96), F32)], axis=1).astype(BF16)
    wg = jnp.zeros((128, 256), F32)
    wg = wg.at[0:16, 0:128].set(gla_w_gate_f[l]).at[16:32, 128:256].set(gla_w_gate_b[l]).astype(BF16)
    bg = jnp.concatenate([gla_b_gate_f[l], gla_b_gate_b[l]])[None, :]
    wo = w_out[l]
    wo_d = wo[768:1024].reshape(4, HEAD_DIM, -1)[_GQA_PERM, :, :].reshape(256, -1)
    w_out_r = jnp.concatenate([wo[0:768], wo_d], axis=0).astype(BF16)

    def tile4(g):
        return jnp.tile(g, 4)[None, :]

    return {
        "w_in": w_in_r, "wg": wg, "bg": bg, "w_out": w_out_r,
        "naq_g": tile4(na_q_norm[l]), "nak_g": tile4(na_k_norm[l]),
        "gq_g": tile4(gqa_q_norm[l]), "gk_g": tile4(gqa_k_norm[l]),
        "gn": tile4(gla_norm[l]),
    }


def kernel(x, c, ctx, c_ctx, w_mod, b_mod, ffn1_w1, ffn1_w3, ffn1_w2, w_in, na_q_norm, na_k_norm, na_rpb,
           gla_w_gate_f, gla_b_gate_f, gla_w_gate_b, gla_b_gate_b, gla_norm, gqa_q_norm, gqa_k_norm, w_out,
           ffn2_w1, ffn2_w3, ffn2_w2):
    nb, s, d = x.shape
    lc = ctx.shape[1]
    depth = w_mod.shape[0]
    tm_x = 512
    tm_z = 256

    cc = jnp.zeros((16, d), F32).at[0:nb].set(c).at[nb].set(c_ctx)
    mod_all = _modulation(cc, w_mod, b_mod)

    consts = {"g256": _group_matrix(256), "g128": _group_matrix(128)}
    consts["cc"], consts["sc"] = _channel_dft_tables()
    cos_t, sin_t = _rope_tables(s)
    dft_x = _dft_tables(s)
    dft_z = _dft_tables(lc)
    z_tab, gla_masks = _gla_tables()
    s_zero = jnp.zeros((nb, 2, 256, 128), F32)

    z = ctx.reshape(1, nb * lc, d)
    four_x = (lambda b, t: (t, b), (s, nb * 256))
    four_z = (lambda b, t: (0, t), (lc, nb * 256))

    for l in range(depth):
        last = l == depth - 1
        mod = mod_all[l, 0:nb + 1].reshape(nb + 1, N_MOD, d)
        mod_x, mod_z = mod[0:nb], mod[nb:nb + 1]
        lw = _layer_weights(l, w_in, na_q_norm, na_k_norm, gla_w_gate_f, gla_b_gate_f, gla_w_gate_b,
                            gla_b_gate_b, gla_norm, gqa_q_norm, gqa_k_norm, w_out)
        f1 = (ffn1_w1[l].astype(BF16), ffn1_w3[l].astype(BF16), ffn1_w2[l].astype(BF16))
        f2 = (ffn2_w1[l].astype(BF16), ffn2_w3[l].astype(BF16), ffn2_w2[l].astype(BF16))
        tb = _na_bias_table(na_rpb[l])

        x = _ffn(x, mod_x, *f1, row0=0, tm=tm_x)
        z = _ffn(z, mod_z, *f1, row0=0, tm=tm_z)

        px = _inproj(x, mod_x, lw, consts, cos_t, sin_t, True, tm_x, four_x)
        pz = _inproj(z, mod_z, lw, consts, cos_t, sin_t, False, tm_z, four_z)
        (ucx, usx, naq_x, nak_x, nav_x, qka_x, gv_x, gr_x, q_x, k_x, v_x) = px
        (ucz, usz, naq_z, nak_z, nav_z, qka_z, gv_z, gr_z, q_z, k_z, v_z) = [
            t if t.ndim == 2 else t.reshape(nb, lc, t.shape[-1]) for t in pz]

        a_x = _dft(dft_x[0], dft_x[1], ucx, usx, tm=1024, tk=512)
        b_x = _na(naq_x, nak_x, nav_x, nak_z, nav_z, tb)
        ofz, obz, s_ctx = _gla(qka_z, gv_z, s_zero, z_tab, gla_masks)
        ofx, obx, _ = _gla(qka_x, gv_x, s_ctx, z_tab, gla_masks)
        d_x = _gqa(q_x, k_x, v_x, k_z, v_z, tq=128)

        x = _outproj(x, mod_x, a_x, four_x[0], b_x, ofx, obx, gr_x, d_x, lw["w_out"], consts["g256"],
                     lw["gn"], tm_x)
        x = _ffn(x, mod_x, *f2, row0=6, tm=tm_x)

        if not last:
            a_z = _dft(dft_z[0], dft_z[1], ucz, usz, tm=lc, tk=lc)
            b_z, d_z = _ctx_attn(naq_z, nak_z, nav_z, q_z, k_z, v_z)
            z = _outproj(z, mod_z, a_z, four_z[0], b_z.reshape(1, nb * lc, 256),
                         ofz.reshape(1, nb * lc, 256), obz.reshape(1, nb * lc, 256),
                         gr_z.reshape(1, nb * lc, 256), d_z.reshape(1, nb * lc, 256),
                         lw["w_out"], consts["g256"], lw["gn"], tm_z)
            z = _ffn(z, mod_z, *f2, row0=6, tm=tm_z)
    return x
```

```python
import functools
import math

import jax
import jax.numpy as jnp
from jax import lax
from jax.experimental import pallas as pl
from jax.experimental.pallas import tpu as pltpu

F32 = jnp.float32
BF16 = jnp.bfloat16

HEAD_DIM = 64
GRID_W = 64
NA_ROWS = 8
NA_COLS = 16
GLA_DK = 32
GLA_TAU = 16.0
GLA_CHUNK = 64
GLA_LEVELS = (32, 16, 8, 4, 2, 1)
ROPE_THETA = 10000.0
EPS = 1e-6
N_MOD = 9
NEG = -1e30

VMEM_LIMIT = 56 * 1024 * 1024


def _cparams(*sem):
    return pltpu.CompilerParams(dimension_semantics=sem, vmem_limit_bytes=VMEM_LIMIT)


def _dot(a, b):
    return jnp.dot(a, b, preferred_element_type=F32)


def _dot_nt(a, b):
    return lax.dot_general(a, b, (((1,), (1,)), ((), ())), preferred_element_type=F32)


def _dot_tn(a, b):
    return lax.dot_general(a, b, (((0,), (0,)), ((), ())), preferred_element_type=F32)


def _sigmoid(x):
    return 1.0 / (1.0 + jnp.exp(-x))


def _silu(x):
    return x * _sigmoid(x)


def _log_sigmoid(x):
    return jnp.minimum(x, 0.0) - jnp.log(1.0 + jnp.exp(-jnp.abs(x)))


def _const_spec(shape):
    nd = len(shape)
    return pl.BlockSpec(shape, lambda *_: (0,) * nd)


def _mod_kernel(c_ref, w_ref, b_ref, o_ref):
    h = _silu(c_ref[...]).astype(BF16)
    o_ref[0] = _dot(h, w_ref[0].astype(BF16)) + b_ref[0]


def _modulation(cc, w_mod, b_mod):
    depth, d, n = w_mod.shape
    rows = cc.shape[0]
    tn = 1024
    return pl.pallas_call(
        _mod_kernel,
        out_shape=jax.ShapeDtypeStruct((depth, rows, n), F32),
        grid=(depth, n // tn),
        in_specs=[
            pl.BlockSpec((rows, d), lambda l, j: (0, 0)),
            pl.BlockSpec((1, d, tn), lambda l, j: (l, 0, j)),
            pl.BlockSpec((1, 1, tn), lambda l, j: (l, 0, j)),
        ],
        out_specs=pl.BlockSpec((1, rows, tn), lambda l, j: (l, 0, j)),
        compiler_params=_cparams("parallel", "parallel"),
        name="adaln_mod",
    )(cc, w_mod, b_mod.reshape(depth, 1, n))


def _norm_mod(x, shift, scale):
    h = x * lax.rsqrt(jnp.mean(x * x, axis=-1, keepdims=True) + EPS)
    return h * (1.0 + scale) + shift


def _swiglu_acc(hb, w1_ref, w3_ref, w2_ref, fc):
    d_ff = w1_ref.shape[1]
    acc = jnp.zeros((hb.shape[0], w2_ref.shape[1]), F32)
    for f0 in range(0, d_ff, fc):
        a = _dot(hb, w1_ref[:, f0:f0 + fc])
        b = _dot(hb, w3_ref[:, f0:f0 + fc])
        g = (_silu(a) * b).astype(BF16)
        acc = acc + _dot(g, w2_ref[f0:f0 + fc, :])
    return acc


def _ffn_kernel(x_ref, mod_ref, w1_ref, w3_ref, w2_ref, o_ref, *, row0, fc):
    x = x_ref[0]
    shift = mod_ref[0, row0:row0 + 1, :]
    scale = mod_ref[0, row0 + 1:row0 + 2, :]
    gate = mod_ref[0, row0 + 2:row0 + 3, :]
    hb = _norm_mod(x, shift, scale).astype(BF16)
    acc = _swiglu_acc(hb, w1_ref, w3_ref, w2_ref, fc)
    o_ref[0] = x + (0.5 * gate) * acc


def _ffn(x, mod, w1, w3, w2, row0, tm):
    nb, r, d = x.shape
    d_ff = w1.shape[1]
    return pl.pallas_call(
        functools.partial(_ffn_kernel, row0=row0, fc=256),
        out_shape=jax.ShapeDtypeStruct(x.shape, F32),
        grid=(nb, r // tm),
        in_specs=[
            pl.BlockSpec((1, tm, d), lambda b, t: (b, t, 0)),
            pl.BlockSpec((1, N_MOD, d), lambda b, t: (b, 0, 0)),
            _const_spec((d, d_ff)),
            _const_spec((d, d_ff)),
            _const_spec((d_ff, d)),
        ],
        out_specs=pl.BlockSpec((1, tm, d), lambda b, t: (b, t, 0)),
        compiler_params=_cparams("parallel", "parallel"),
        name="ffn_halfstep",
    )(x, mod, w1, w3, w2)


def _group_sum(xsq, gmat):
    hi = xsq.astype(BF16)
    lo = (xsq - hi.astype(F32)).astype(BF16)
    return _dot(hi, gmat) + _dot(lo, gmat)


def _head_rms(x, gmat, gain):
    ms = _group_sum(x * x, gmat) * (1.0 / HEAD_DIM)
    return x * lax.rsqrt(ms + EPS) * gain


def _rope128(x, cos, sin_signed):
    lane = lax.broadcasted_iota(jnp.int32, x.shape, 1)
    first_half = (lane % HEAD_DIM) < (HEAD_DIM // 2)
    partner = jnp.where(first_half, pltpu.roll(x, 96, 1), pltpu.roll(x, 32, 1))
    return x * cos + partner * sin_signed


def _inproj_kernel(x_ref, mod_ref, w_ref, g256_ref, g128_ref, cc_ref, sc_ref,
                   naq_g_ref, nak_g_ref, gq_g_ref, gk_g_ref, wg_ref, bg_ref, cos_ref, sin_ref,
                   uc_ref, us_ref, naq_ref, nak_ref, nav_ref, qka_ref, gv_ref, gr_ref,
                   q_ref, k_ref, v_ref, *, rope):
    x = x_ref[0]
    shift = mod_ref[0, 3:4, :]
    scale = mod_ref[0, 4:5, :]
    hb = _norm_mod(x, shift, scale).astype(BF16)
    h = _dot(hb, w_ref[...])
    g256 = g256_ref[...]
    g128 = g128_ref[...]
    ub = h[:, 0:256].astype(BF16)
    uc_ref[...] = _dot(ub, cc_ref[...]).astype(BF16)
    us_ref[...] = _dot(ub, sc_ref[...]).astype(BF16)
    naq_ref[0] = (_head_rms(h[:, 256:512], g256, naq_g_ref[...]) * HEAD_DIM ** -0.5).astype(BF16)
    nak_ref[0] = _head_rms(h[:, 512:768], g256, nak_g_ref[...]).astype(BF16)
    nav_ref[0] = h[:, 768:1024].astype(BF16)
    gates = _dot(h[:, 2304:2432].astype(BF16), wg_ref[...]) + bg_ref[...]
    log_a = _log_sigmoid(gates) / GLA_TAU
    qka_ref[0, :, 0:128] = h[:, 1024:1152] * GLA_DK ** -0.5
    qka_ref[0, :, 128:256] = h[:, 1152:1280]
    qka_ref[0, :, 256:512] = log_a
    gv_ref[0] = h[:, 1280:1536].astype(BF16)
    gr_ref[0] = h[:, 1536:1792]
    q = _head_rms(h[:, 1792:2048], g256, gq_g_ref[...])
    k = _head_rms(h[:, 2048:2176], g128, gk_g_ref[:, 0:128])
    if rope:
        cos = cos_ref[...]
        sin = sin_ref[...]
        q = jnp.concatenate([_rope128(q[:, 0:128], cos, sin), _rope128(q[:, 128:256], cos, sin)], axis=1)
        k = _rope128(k, cos, sin)
    q_ref[0] = (q * HEAD_DIM ** -0.5).astype(BF16)
    k_ref[0] = k.astype(BF16)
    v_ref[0] = h[:, 2176:2304].astype(BF16)


def _inproj(x, mod, lw, consts, cos_t, sin_t, rope, tm, four_index):
    nb, r, d = x.shape
    n_four = four_index[1]

    def act(width, dtype):
        return jax.ShapeDtypeStruct((nb, r, width), dtype)

    def act_spec(width):
        return pl.BlockSpec((1, tm, width), lambda b, t: (b, t, 0))

    four_shape = jax.ShapeDtypeStruct(n_four, BF16)
    four_spec = pl.BlockSpec((tm, 256), four_index[0])
    vec256 = _const_spec((1, 256))
    out_shapes = [four_shape, four_shape, act(256, BF16), act(256, BF16), act(256, BF16),
                  act(512, F32), act(256, BF16), act(256, F32),
                  act(256, BF16), act(128, BF16), act(128, BF16)]
    out_specs = [four_spec, four_spec, act_spec(256), act_spec(256), act_spec(256),
                 act_spec(512), act_spec(256), act_spec(256),
                 act_spec(256), act_spec(128), act_spec(128)]
    return pl.pallas_call(
        functools.partial(_inproj_kernel, rope=rope),
        out_shape=out_shapes,
        grid=(nb, r // tm),
        in_specs=[
            pl.BlockSpec((1, tm, d), lambda b, t: (b, t, 0)),
            pl.BlockSpec((1, N_MOD, d), lambda b, t: (b, 0, 0)),
            _const_spec(lw["w_in"].shape),
            _const_spec((256, 256)), _const_spec((128, 128)),
            _const_spec((256, 256)), _const_spec((256, 256)),
            vec256, vec256, vec256, vec256,
            _const_spec((128, 256)), vec256,
            pl.BlockSpec((tm, 128), lambda b, t: (t, 0)),
            pl.BlockSpec((tm, 128), lambda b, t: (t, 0)),
        ],
        out_specs=out_specs,
        compiler_params=_cparams("parallel", "parallel"),
        name="in_proj",
    )(x, mod, lw["w_in"], consts["g256"], consts["g128"], consts["cc"], consts["sc"],
      lw["naq_g"], lw["nak_g"], lw["gq_g"], lw["gk_g"], lw["wg"], lw["bg"], cos_t, sin_t)


def _dft_kernel(c_ref, s_ref, uc_ref, us_ref, o_ref, acc_ref):
    k = pl.program_id(1)

    @pl.when(k == 0)
    def _():
        acc_ref[...] = jnp.zeros_like(acc_ref)

    acc_ref[...] += _dot(c_ref[...], uc_ref[...]) - _dot(s_ref[...], us_ref[...])

    @pl.when(k == pl.num_programs(1) - 1)
    def _():
        o_ref[...] = acc_ref[...].astype(o_ref.dtype)


def _dft(c_tab, s_tab, uc, us, tm, tk):
    l, n = uc.shape
    return pl.pallas_call(
        _dft_kernel,
        out_shape=jax.ShapeDtypeStruct((l, n), BF16),
        grid=(l // tm, l // tk),
        in_specs=[
            pl.BlockSpec((tm, tk), lambda i, k: (i, k)),
            pl.BlockSpec((tm, tk), lambda i, k: (i, k)),
            pl.BlockSpec((tk, n), lambda i, k: (k, 0)),
            pl.BlockSpec((tk, n), lambda i, k: (k, 0)),
        ],
        out_specs=pl.BlockSpec((tm, n), lambda i, k: (i, 0)),
        scratch_shapes=[pltpu.VMEM((tm, n), F32)],
        compiler_params=_cparams("parallel", "arbitrary"),
        name="position_dft",
    )(c_tab, s_tab, uc, us)


def _stack_heads(q, n_heads, width):
    lane = lax.broadcasted_iota(jnp.int32, q.shape, 1)
    zero = jnp.zeros_like(q)
    return jnp.concatenate([jnp.where(lane // width == h, q, zero) for h in range(n_heads)], axis=0)


def _unstack_heads(o, n_heads, width):
    m = o.shape[0] // n_heads
    lane = lax.broadcasted_iota(jnp.int32, (m, o.shape[1]), 1)
    out = jnp.zeros((m, o.shape[1]), F32)
    for h in range(n_heads):
        out = out + jnp.where(lane // width == h, o[h * m:(h + 1) * m, :], 0.0)
    return out


def _softmax_pv(score_blocks, value_blocks):
    m = score_blocks[0].max(axis=-1, keepdims=True)
    for s in score_blocks[1:]:
        m = jnp.maximum(m, s.max(axis=-1, keepdims=True))
    l = None
    o = None
    for s, v in zip(score_blocks, value_blocks):
        p = jnp.exp(s - m)
        ps = p.sum(axis=-1, keepdims=True)
        pv = _dot(p.astype(BF16), v)
        l = ps if l is None else l + ps
        o = pv if o is None else o + pv
    return o * (1.0 / l)


def _na_kernel(q_ref, k_ref, v_ref, kc_ref, vc_ref, tb_ref, o_ref, *, rows_per_step, n_rows):
    rb = pl.program_id(1)
    kc = kc_ref[0]
    vc = vc_ref[0]
    for i in range(rows_per_step):
        r = rb * rows_per_step + i
        rs = jnp.clip(r - NA_ROWS // 2, 0, n_rows - NA_ROWS)
        off = r - rs
        q = q_ref[0, i * GRID_W:(i + 1) * GRID_W, :]
        qs = _stack_heads(q, 4, HEAD_DIM)
        start = pl.multiple_of(rs * GRID_W, GRID_W)
        kw = k_ref[0, pl.ds(start, NA_ROWS * GRID_W), :]
        vw = v_ref[0, pl.ds(start, NA_ROWS * GRID_W), :]
        s_w = _dot_nt(qs, kw) + tb_ref[off]
        s_c = _dot_nt(qs, kc)
        o = _softmax_pv([s_w, s_c], [vw, vc])
        o_ref[0, i * GRID_W:(i + 1) * GRID_W, :] = _unstack_heads(o, 4, HEAD_DIM).astype(BF16)


def _na(q, k, v, kc, vc, tb):
    nb, s, w = q.shape
    n_rows = s // GRID_W
    rps = 8
    lc = kc.shape[1]
    return pl.pallas_call(
        functools.partial(_na_kernel, rows_per_step=rps, n_rows=n_rows),
        out_shape=jax.ShapeDtypeStruct((nb, s, w), BF16),
        grid=(nb, n_rows // rps),
        in_specs=[
            pl.BlockSpec((1, rps * GRID_W, w), lambda b, t: (b, t, 0)),
            pl.BlockSpec((1, s, w), lambda b, t: (b, 0, 0)),
            pl.BlockSpec((1, s, w), lambda b, t: (b, 0, 0)),
            pl.BlockSpec((1, lc, w), lambda b, t: (b, 0, 0)),
            pl.BlockSpec((1, lc, w), lambda b, t: (b, 0, 0)),
            _const_spec(tb.shape),
        ],
        out_specs=pl.BlockSpec((1, rps * GRID_W, w), lambda b, t: (b, t, 0)),
        compiler_params=_cparams("parallel", "parallel"),
        name="neighborhood_attn",
    )(q, k, v, kc, vc, tb)


def _gqa_scores_lhs(q):
    return jnp.concatenate([_stack_heads(q[:, 0:128], 2, HEAD_DIM),
                            _stack_heads(q[:, 128:256], 2, HEAD_DIM)], axis=0)


def _gqa_merge(o):
    m = o.shape[0] // 4
    return jnp.concatenate([_unstack_heads(o[0:2 * m], 2, HEAD_DIM),
                            _unstack_heads(o[2 * m:4 * m], 2, HEAD_DIM)], axis=1)


def _gqa_kernel(q_ref, k_ref, v_ref, kc_ref, vc_ref, o_ref, *, kb):
    qs = _gqa_scores_lhs(q_ref[0])
    n = qs.shape[0]
    m = jnp.full((n, 1), NEG, F32)
    l = jnp.zeros((n, 1), F32)
    acc = jnp.zeros((n, 128), F32)
    blocks = [(k_ref, v_ref, j * kb, kb) for j in range(k_ref.shape[1] // kb)]
    blocks.append((kc_ref, vc_ref, 0, kc_ref.shape[1]))
    for kr, vr, k0, size in blocks:
        s = _dot_nt(qs, kr[0, k0:k0 + size, :])
        m_new = jnp.maximum(m, s.max(axis=-1, keepdims=True))
        alpha = jnp.exp(m - m_new)
        p = jnp.exp(s - m_new)
        l = l * alpha + p.sum(axis=-1, keepdims=True)
        acc = acc * alpha + _dot(p.astype(BF16), vr[0, k0:k0 + size, :])
        m = m_new
    o_ref[0] = _gqa_merge(acc * (1.0 / l)).astype(BF16)


def _gqa(q, k, v, kc, vc, tq):
    nb, s, _ = q.shape
    lc = kc.shape[1]
    return pl.pallas_call(
        functools.partial(_gqa_kernel, kb=512),
        out_shape=jax.ShapeDtypeStruct((nb, s, 256), BF16),
        grid=(nb, s // tq),
        in_specs=[
            pl.BlockSpec((1, tq, 256), lambda b, t: (b, t, 0)),
            pl.BlockSpec((1, s, 128), lambda b, t: (b, 0, 0)),
            pl.BlockSpec((1, s, 128), lambda b, t: (b, 0, 0)),
            pl.BlockSpec((1, lc, 128), lambda b, t: (b, 0, 0)),
            pl.BlockSpec((1, lc, 128), lambda b, t: (b, 0, 0)),
        ],
        out_specs=pl.BlockSpec((1, tq, 256), lambda b, t: (b, t, 0)),
        compiler_params=_cparams("parallel", "parallel"),
        name="gqa_attn",
    )(q, k, v, kc, vc)


def _ctx_attn_kernel(naq_ref, nak_ref, nav_ref, q_ref, k_ref, v_ref, bz_ref, dz_ref):
    qs = _stack_heads(naq_ref[0], 4, HEAD_DIM)
    o = _softmax_pv([_dot_nt(qs, nak_ref[0])], [nav_ref[0]])
    bz_ref[0] = _unstack_heads(o, 4, HEAD_DIM).astype(BF16)
    qg = _gqa_scores_lhs(q_ref[0])
    og = _softmax_pv([_dot_nt(qg, k_ref[0])], [v_ref[0]])
    dz_ref[0] = _gqa_merge(og).astype(BF16)


def _ctx_attn(naq, nak, nav, q, k, v):
    nb, lc, _ = naq.shape

    def spec(w):
        return pl.BlockSpec((1, lc, w), lambda b: (b, 0, 0))

    out = jax.ShapeDtypeStruct((nb, lc, 256), BF16)
    return pl.pallas_call(
        _ctx_attn_kernel,
        out_shape=[out, out],
        grid=(nb,),
        in_specs=[spec(256), spec(256), spec(256), spec(256), spec(128), spec(128)],
        out_specs=[spec(256), spec(256)],
        compiler_params=_cparams("parallel"),
        name="context_attn",
    )(naq, nak, nav, q, k, v)


def _gla_kernel(qf_ref, vf_ref, qb_ref, vb_ref, z_ref, mask_ref, s0_ref, of_ref, ob_ref, sfin_ref,
                st_ref, *, chunks):
    p = pl.program_id(1)
    c = GLA_CHUNK

    @pl.when(p == 0)
    def _():
        st_ref[...] = s0_ref[0]

    row = lax.broadcasted_iota(jnp.int32, (256, 128), 0)
    lane = lax.broadcasted_iota(jnp.int32, (256, 128), 1)
    bd_mask = (row // HEAD_DIM) == (lane // GLA_DK)

    def one(d, in_ref, v_ref, o_ref, r0, a0):
        q = in_ref[0, r0:r0 + c, 0:128]
        k = in_ref[0, r0:r0 + c, 128:256]
        a = in_ref[0, r0:r0 + c, a0:a0 + 128]
        v = v_ref[0, r0:r0 + c, :]
        a_hi = a.astype(BF16)
        r1 = a - a_hi.astype(F32)
        a_mid = r1.astype(BF16)
        a_lo = (r1 - a_mid.astype(F32)).astype(BF16)
        e3 = _dot(z_ref[d], jnp.concatenate([a_hi, a_mid, a_lo], axis=1))
        e = e3[:, 0:128] + e3[:, 128:256] + e3[:, 256:384]
        x = jnp.exp(e)
        qe = (q * x[0:c]).astype(BF16)
        ke = (k * x[c:2 * c]).astype(BF16)
        decay = x[8 * c:8 * c + 1, :]
        st = st_ref[d]
        o = _dot_nt(qe, st.astype(BF16))
        row4 = lax.broadcasted_iota(jnp.int32, (4 * c, 128), 0)
        lane4 = lax.broadcasted_iota(jnp.int32, (4 * c, 128), 1)
        head_mask = (row4 // c) == (lane4 // GLA_DK)
        scores = jnp.zeros((4 * c, c), F32)
        n_lev = len(GLA_LEVELS)
        for lev in range(n_lev + 1):
            if lev < n_lev:
                xl = x[(2 + lev) * c:(3 + lev) * c]
                ql = q * xl
                kl = (k * xl).astype(BF16)
            else:
                ql = q
                kl = k.astype(BF16)
            qs = jnp.where(head_mask, jnp.concatenate([ql, ql, ql, ql], axis=0), 0.0).astype(BF16)
            scores = scores + _dot_nt(qs, kl) * mask_ref[d, lev]
        o_all = _dot(scores.astype(BF16), v)
        o_ref[0, r0:r0 + c, :] = o + _unstack_heads(o_all, 4, HEAD_DIM)
        upd = _dot_tn(v, ke)
        st_ref[d] = st * decay + jnp.where(bd_mask, upd, 0.0)

    for ci in range(chunks):
        one(0, qf_ref, vf_ref, of_ref, ci * c, 256)
        one(1, qb_ref, vb_ref, ob_ref, (chunks - 1 - ci) * c, 384)

    @pl.when(p == pl.num_programs(1) - 1)
    def _():
        sfin_ref[0] = st_ref[...]


def _gla(qka, v, s0, z_tab, masks):
    nb, r, _ = qka.shape
    chunks = 4
    blk = chunks * GLA_CHUNK
    nblk = r // blk
    out = jax.ShapeDtypeStruct((nb, r, 256), F32)
    return pl.pallas_call(
        functools.partial(_gla_kernel, chunks=chunks),
        out_shape=[out, out, jax.ShapeDtypeStruct((nb, 2, 256, 128), F32)],
        grid=(nb, nblk),
        in_specs=[
            pl.BlockSpec((1, blk, 512), lambda b, p: (b, p, 0)),
            pl.BlockSpec((1, blk, 256), lambda b, p: (b, p, 0)),
            pl.BlockSpec((1, blk, 512), lambda b, p: (b, nblk - 1 - p, 0)),
            pl.BlockSpec((1, blk, 256), lambda b, p: (b, nblk - 1 - p, 0)),
            _const_spec(z_tab.shape),
            _const_spec(masks.shape),
            pl.BlockSpec((1, 2, 256, 128), lambda b, p: (b, 0, 0, 0)),
        ],
        out_specs=[
            pl.BlockSpec((1, blk, 256), lambda b, p: (b, p, 0)),
            pl.BlockSpec((1, blk, 256), lambda b, p: (b, nblk - 1 - p, 0)),
            pl.BlockSpec((1, 2, 256, 128), lambda b, p: (b, 0, 0, 0)),
        ],
        scratch_shapes=[pltpu.VMEM((2, 256, 128), F32)],
        compiler_params=_cparams("parallel", "arbitrary"),
        name="gla_scan",
    )(qka, v, qka, v, z_tab, masks, s0)


def _outproj_kernel(x_ref, mod_ref, a_ref, b_ref, of_ref, ob_ref, r_ref, d_ref, w_ref, g256_ref,
                    gn_ref, o_ref):
    o = of_ref[0] + ob_ref[0]
    cx = (_head_rms(o, g256_ref[...], gn_ref[...]) * _silu(r_ref[0])).astype(BF16)
    acc = _dot(a_ref[...], w_ref[0:256, :])
    acc = acc + _dot(b_ref[0], w_ref[256:512, :])
    acc = acc + _dot(cx, w_ref[512:768, :])
    acc = acc + _dot(d_ref[0], w_ref[768:1024, :])
    o_ref[0] = x_ref[0] + mod_ref[0, 5:6, :] * acc


def _outproj(x, mod, a, a_index, b, o_f, o_b, r, dd, w_out, g256, gn, tm):
    nb, rows, d = x.shape

    def act_spec(width):
        return pl.BlockSpec((1, tm, width), lambda bb, t: (bb, t, 0))

    return pl.pallas_call(
        _outproj_kernel,
        out_shape=jax.ShapeDtypeStruct(x.shape, F32),
        grid=(nb, rows // tm),
        in_specs=[
            act_spec(d),
            pl.BlockSpec((1, N_MOD, d), lambda bb, t: (bb, 0, 0)),
            pl.BlockSpec((tm, 256), a_index),
            act_spec(256), act_spec(256), act_spec(256), act_spec(256), act_spec(256),
            _const_spec(w_out.shape), _const_spec((256, 256)), _const_spec((1, 256)),
        ],
        out_specs=act_spec(d),
        compiler_params=_cparams("parallel", "parallel"),
        name="out_proj",
    )(x, mod, a, b, o_f, o_b, r, dd, w_out, g256, gn)


def _dft_tables(l):
    n = jnp.arange(l, dtype=jnp.int32)
    ang = ((n[:, None] * n[None, :]) % l).astype(F32) * (2.0 * math.pi / l)
    scale = l ** -0.5
    return (jnp.cos(ang) * scale).astype(BF16), (jnp.sin(ang) * scale).astype(BF16)


def _channel_dft_tables():
    n = jnp.arange(256, dtype=jnp.int32)
    same = (n[:, None] // HEAD_DIM) == (n[None, :] // HEAD_DIM)
    ang = (((n[:, None] % HEAD_DIM) * (n[None, :] % HEAD_DIM)) % HEAD_DIM).astype(F32) * (2.0 * math.pi / HEAD_DIM)
    scale = HEAD_DIM ** -0.5
    cc = jnp.where(same, jnp.cos(ang) * scale, 0.0).astype(BF16)
    sc = jnp.where(same, jnp.sin(ang) * scale, 0.0).astype(BF16)
    return cc, sc


def _group_matrix(width):
    n = jnp.arange(width, dtype=jnp.int32)
    return ((n[:, None] // HEAD_DIM) == (n[None, :] // HEAD_DIM)).astype(BF16)


def _rope_tables(n_tokens):
    t = jnp.arange(n_tokens, dtype=jnp.int32)
    row = (t // GRID_W).astype(F32)
    col = (t % GRID_W).astype(F32)
    n_freq = HEAD_DIM // 4
    inv_freq = ROPE_THETA ** (-jnp.arange(n_freq, dtype=F32) / n_freq)
    ang = jnp.concatenate([row[:, None] * inv_freq, col[:, None] * inv_freq], axis=-1)
    cos, sin = jnp.cos(ang), jnp.sin(ang)
    cos_t = jnp.concatenate([cos, cos, cos, cos], axis=-1)
    sin_t = jnp.concatenate([-sin, sin, -sin, sin], axis=-1)
    return cos_t, sin_t


def _gla_tables():
    c = GLA_CHUNK
    i = jnp.arange(c, dtype=jnp.int32)[:, None]
    t = jnp.arange(c, dtype=jnp.int32)[None, :]
    ones_row = jnp.ones((16, c), jnp.bool_)

    def tables(forward):
        z = [t <= i, t > i] if forward else [t >= i, t < i]
        m = []
        for s in GLA_LEVELS:
            pos = i % (2 * s)
            base = i - pos
            bound = base + s - 1
            right = pos >= s
            if forward:
                zq = right & (t > bound) & (t <= i)
                zk = (~right) & (t > i) & (t <= bound)
            else:
                zq = (~right) & (t >= i) & (t <= bound)
                zk = right & (t > bound) & (t < i)
            z.append(zq | zk)
            same_parent = (i // (2 * s)) == (t // (2 * s))
            t_right = (t % (2 * s)) >= s
            if forward:
                m.append(same_parent & right & (~t_right))
            else:
                m.append(same_parent & (~right) & t_right)
        m.append(i == t)
        z.append(ones_row)
        zt = jnp.concatenate(z, axis=0).astype(BF16)
        mt = jnp.stack([jnp.concatenate([mm, mm, mm, mm], axis=0) for mm in m]).astype(F32)
        return zt, mt

    zf, mf = tables(True)
    zb, mb = tables(False)
    return jnp.stack([zf, zb]), jnp.stack([mf, mb])


def _na_bias_table(rpb):
    c = jnp.arange(GRID_W)[:, None, None]
    kc = jnp.arange(GRID_W)[None, :, None]
    m = jnp.arange(2 * NA_COLS - 1)[None, None, :]
    cs = jnp.clip(c - NA_COLS // 2, 0, GRID_W - NA_COLS)
    valid = (kc >= cs) & (kc < cs + NA_COLS)
    pick = (valid & (kc - c + NA_COLS - 1 == m)).astype(F32)
    by_row = jnp.einsum('hrm,ckm->rhck', rpb.astype(F32), pick, precision=lax.Precision.HIGHEST)
    by_row = jnp.where(valid[None, None, :, :, 0], by_row, NEG)
    blocks = [jnp.concatenate([by_row[j - off + NA_ROWS - 1] for j in range(NA_ROWS)], axis=-1)
              for off in range(NA_ROWS)]
    return jnp.stack(blocks).reshape(NA_ROWS, 4 * GRID_W, NA_ROWS * GRID_W)


_GQA_PERM = (0, 2, 1, 3)


def _layer_weights(l, w_in, na_q_norm, na_k_norm, gla_w_gate_f, gla_b_gate_f, gla_w_gate_b, gla_b_gate_b,
                   gla_norm, gqa_q_norm, gqa_k_norm, w_out):
    wi = w_in[l]
    d = wi.shape[0]
    gq = wi[:, 1824:2080].reshape(d, 4, HEAD_DIM)[:, _GQA_PERM, :].reshape(d, 256)
    w_in_r = jnp.concatenate([wi[:, 0:1792], gq, wi[:, 2080:2336], wi[:, 1792:1824],
                              jnp.zeros((d, 96), F32)], axis=1).astype(BF16)
    wg = jnp.zeros((128, 256), F32)
    wg = wg.at[0:16, 0:128].set(gla_w_gate_f[l]).at[16:32, 128:256].set(gla_w_gate_b[l]).astype(BF16)
    bg = jnp.concatenate([gla_b_gate_f[l], gla_b_gate_b[l]])[None, :]
    wo = w_out[l]
    wo_d = wo[768:1024].reshape(4, HEAD_DIM, -1)[_GQA_PERM, :, :].reshape(256, -1)
    w_out_r = jnp.concatenate([wo[0:768], wo_d], axis=0).astype(BF16)

    def tile4(g):
        return jnp.tile(g, 4)[None, :]

    return {
        "w_in": w_in_r, "wg": wg, "bg": bg, "w_out": w_out_r,
        "naq_g": tile4(na_q_norm[l]), "nak_g": tile4(na_k_norm[l]),
        "gq_g": tile4(gqa_q_norm[l]), "gk_g": tile4(gqa_k_norm[l]),
        "gn": tile4(gla_norm[l]),
    }


def kernel(x, c, ctx, c_ctx, w_mod, b_mod, ffn1_w1, ffn1_w3, ffn1_w2, w_in, na_q_norm, na_k_norm, na_rpb,
           gla_w_gate_f, gla_b_gate_f, gla_w_gate_b, gla_b_gate_b, gla_norm, gqa_q_norm, gqa_k_norm, w_out,
           ffn2_w1, ffn2_w3, ffn2_w2):
    nb, s, d = x.shape
    lc = ctx.shape[1]
    depth = w_mod.shape[0]
    tm_x = 512
    tm_z = 256

    cc = jnp.zeros((16, d), F32).at[0:nb].set(c).at[nb].set(c_ctx)
    mod_all = _modulation(cc, w_mod, b_mod)

    consts = {"g256": _group_matrix(256), "g128": _group_matrix(128)}
    consts["cc"], consts["sc"] = _channel_dft_tables()
    cos_t, sin_t = _rope_tables(s)
    dft_x = _dft_tables(s)
    dft_z = _dft_tables(lc)
    z_tab, gla_masks = _gla_tables()
    s_zero = jnp.zeros((nb, 2, 256, 128), F32)

    z = ctx.reshape(1, nb * lc, d)
    four_x = (lambda b, t: (t, b), (s, nb * 256))
    four_z = (lambda b, t: (0, t), (lc, nb * 256))

    for l in range(depth):
        last = l == depth - 1
        mod = mod_all[l, 0:nb + 1].reshape(nb + 1, N_MOD, d)
        mod_x, mod_z = mod[0:nb], mod[nb:nb + 1]
        lw = _layer_weights(l, w_in, na_q_norm, na_k_norm, gla_w_gate_f, gla_b_gate_f, gla_w_gate_b,
                            gla_b_gate_b, gla_norm, gqa_q_norm, gqa_k_norm, w_out)
        f1 = (ffn1_w1[l].astype(BF16), ffn1_w3[l].astype(BF16), ffn1_w2[l].astype(BF16))
        f2 = (ffn2_w1[l].astype(BF16), ffn2_w3[l].astype(BF16), ffn2_w2[l].astype(BF16))
        tb = _na_bias_table(na_rpb[l])

        x = _ffn(x, mod_x, *f1, row0=0, tm=tm_x)
        z = _ffn(z, mod_z, *f1, row0=0, tm=tm_z)

        px = _inproj(x, mod_x, lw, consts, cos_t, sin_t, True, tm_x, four_x)
        pz = _inproj(z, mod_z, lw, consts, cos_t, sin_t, False, tm_z, four_z)
        (ucx, usx, naq_x, nak_x, nav_x, qka_x, gv_x, gr_x, q_x, k_x, v_x) = px
        (ucz, usz, naq_z, nak_z, nav_z, qka_z, gv_z, gr_z, q_z, k_z, v_z) = [
            t if t.ndim == 2 else t.reshape(nb, lc, t.shape[-1]) for t in pz]

        a_x = _dft(dft_x[0], dft_x[1], ucx, usx, tm=1024, tk=512)
        b_x = _na(naq_x, nak_x, nav_x, nak_z, nav_z, tb)
        ofz, obz, s_ctx = _gla(qka_z, gv_z, s_zero, z_tab, gla_masks)
        ofx, obx, _ = _gla(qka_x, gv_x, s_ctx, z_tab, gla_masks)
        d_x = _gqa(q_x, k_x, v_x, k_z, v_z, tq=128)

        x = _outproj(x, mod_x, a_x, four_x[0], b_x, ofx, obx, gr_x, d_x, lw["w_out"], consts["g256"],
                     lw["gn"], tm_x)
        x = _ffn(x, mod_x, *f2, row0=6, tm=tm_x)

        if not last:
            a_z = _dft(dft_z[0], dft_z[1], ucz, usz, tm=lc, tk=lc)
            b_z, d_z = _ctx_attn(naq_z, nak_z, nav_z, q_z, k_z, v_z)
            z = _outproj(z, mod_z, a_z, four_z[0], b_z.reshape(1, nb * lc, 256),
                         ofz.reshape(1, nb * lc, 256), obz.reshape(1, nb * lc, 256),
                         gr_z.reshape(1, nb * lc, 256), d_z.reshape(1, nb * lc, 256),
                         lw["w_out"], consts["g256"], lw["gn"], tm_z)
            z = _ffn(z, mod_z, *f2, row0=6, tm=tm_z)
    return x
```

```python
import functools
import math

import jax
import jax.numpy as jnp
from jax import lax
from jax.experimental import pallas as pl
from jax.experimental.pallas import tpu as pltpu

F32 = jnp.float32
BF16 = jnp.bfloat16

HEAD_DIM = 64
GRID_W = 64
NA_ROWS = 8
NA_COLS = 16
GLA_DK = 32
GLA_TAU = 16.0
GLA_CHUNK = 64
GLA_LEVELS = (32, 16, 8, 4, 2, 1)
ROPE_THETA = 10000.0
EPS = 1e-6
N_MOD = 9
NEG = -1e30

VMEM_LIMIT = 56 * 1024 * 1024


def _cparams(*sem):
    return pltpu.CompilerParams(dimension_semantics=sem, vmem_limit_bytes=VMEM_LIMIT)


def _dot(a, b):
    return jnp.dot(a, b, preferred_element_type=F32)


def _dot_nt(a, b):
    return lax.dot_general(a, b, (((1,), (1,)), ((), ())), preferred_element_type=F32)


def _dot_tn(a, b):
    return lax.dot_general(a, b, (((0,), (0,)), ((), ())), preferred_element_type=F32)


def _sigmoid(x):
    return 1.0 / (1.0 + jnp.exp(-x))


def _silu(x):
    return x * _sigmoid(x)


def _log_sigmoid(x):
    return jnp.minimum(x, 0.0) - jnp.log(1.0 + jnp.exp(-jnp.abs(x)))


def _const_spec(shape):
    nd = len(shape)
    return pl.BlockSpec(shape, lambda *_: (0,) * nd, pipeline_mode=pl.Buffered(1))


def _mod_kernel(c_ref, w_ref, b_ref, o_ref):
    h = _silu(c_ref[...]).astype(BF16)
    o_ref[0] = _dot(h, w_ref[0].astype(BF16)) + b_ref[0]


def _modulation(cc, w_mod, b_mod):
    depth, d, n = w_mod.shape
    rows = cc.shape[0]
    tn = 1024
    return pl.pallas_call(
        _mod_kernel,
        out_shape=jax.ShapeDtypeStruct((depth, rows, n), F32),
        grid=(depth, n // tn),
        in_specs=[
            pl.BlockSpec((rows, d), lambda l, j: (0, 0)),
            pl.BlockSpec((1, d, tn), lambda l, j: (l, 0, j)),
            pl.BlockSpec((1, 1, tn), lambda l, j: (l, 0, j)),
        ],
        out_specs=pl.BlockSpec((1, rows, tn), lambda l, j: (l, 0, j)),
        compiler_params=_cparams("parallel", "parallel"),
        name="adaln_mod",
    )(cc, w_mod, b_mod.reshape(depth, 1, n))


def _norm_mod(x, shift, scale):
    h = x * lax.rsqrt(jnp.mean(x * x, axis=-1, keepdims=True) + EPS)
    return h * (1.0 + scale) + shift


def _swiglu_acc(hb, w1_ref, w3_ref, w2_ref, fc):
    d_ff = w1_ref.shape[1]
    acc = jnp.zeros((hb.shape[0], w2_ref.shape[1]), F32)
    for f0 in range(0, d_ff, fc):
        a = _dot(hb, w1_ref[:, f0:f0 + fc])
        b = _dot(hb, w3_ref[:, f0:f0 + fc])
        g = (_silu(a) * b).astype(BF16)
        acc = acc + _dot(g, w2_ref[f0:f0 + fc, :])
    return acc


def _group_sum(xsq, gmat):
    hi = xsq.astype(BF16)
    lo = (xsq - hi.astype(F32)).astype(BF16)
    return _dot(hi, gmat) + _dot(lo, gmat)


def _head_rms(x, gmat, gain):
    ms = _group_sum(x * x, gmat) * (1.0 / HEAD_DIM)
    return x * lax.rsqrt(ms + EPS) * gain


def _rope128(x, cos, sin_signed):
    lane = lax.broadcasted_iota(jnp.int32, x.shape, 1)
    first_half = (lane % HEAD_DIM) < (HEAD_DIM // 2)
    partner = jnp.where(first_half, pltpu.roll(x, 96, 1), pltpu.roll(x, 32, 1))
    return x * cos + partner * sin_signed


def _inproj_kernel(x_ref, mod_ref, w1_ref, w3_ref, w2_ref, w_ref, g256_ref, g128_ref, cc_ref, sc_ref,
                   naq_g_ref, nak_g_ref, gq_g_ref, gk_g_ref, wg_ref, bg_ref, cos_ref, sin_ref,
                   x1_ref, uc_ref, us_ref, naq_ref, nak_ref, nav_ref, qka_ref, gv_ref, gr_ref,
                   q_ref, k_ref, v_ref, *, rope, fc):
    x0 = x_ref[0]
    hb0 = _norm_mod(x0, mod_ref[0, 0:1, :], mod_ref[0, 1:2, :]).astype(BF16)
    x = x0 + (0.5 * mod_ref[0, 2:3, :]) * _swiglu_acc(hb0, w1_ref, w3_ref, w2_ref, fc)
    x1_ref[0] = x
    shift = mod_ref[0, 3:4, :]
    scale = mod_ref[0, 4:5, :]
    hb = _norm_mod(x, shift, scale).astype(BF16)
    h = _dot(hb, w_ref[...])
    g256 = g256_ref[...]
    g128 = g128_ref[...]
    ub = h[:, 0:256].astype(BF16)
    uc_ref[...] = _dot(ub, cc_ref[...]).astype(BF16)
    us_ref[...] = _dot(ub, sc_ref[...]).astype(BF16)
    naq_ref[0] = (_head_rms(h[:, 256:512], g256, naq_g_ref[...]) * HEAD_DIM ** -0.5).astype(BF16)
    nak_ref[0] = _head_rms(h[:, 512:768], g256, nak_g_ref[...]).astype(BF16)
    nav_ref[0] = h[:, 768:1024].astype(BF16)
    gates = _dot(h[:, 2304:2432].astype(BF16), wg_ref[...]) + bg_ref[...]
    log_a = _log_sigmoid(gates) / GLA_TAU
    qka_ref[0, :, 0:128] = h[:, 1024:1152] * GLA_DK ** -0.5
    qka_ref[0, :, 128:256] = h[:, 1152:1280]
    qka_ref[0, :, 256:512] = log_a
    gv_ref[0] = h[:, 1280:1536].astype(BF16)
    gr_ref[0] = h[:, 1536:1792]
    q = _head_rms(h[:, 1792:2048], g256, gq_g_ref[...])
    k = _head_rms(h[:, 2048:2176], g128, gk_g_ref[:, 0:128])
    if rope:
        cos = cos_ref[...]
        sin = sin_ref[...]
        q = jnp.concatenate([_rope128(q[:, 0:128], cos, sin), _rope128(q[:, 128:256], cos, sin)], axis=1)
        k = _rope128(k, cos, sin)
    q_ref[0] = (q * HEAD_DIM ** -0.5).astype(BF16)
    k_ref[0] = k.astype(BF16)
    v_ref[0] = h[:, 2176:2304].astype(BF16)


def _ffn_inproj(x, mod, ffn_w, lw, consts, cos_t, sin_t, rope, tm, four_index):
    nb, r, d = x.shape
    n_four = four_index[1]
    w1, w3, w2 = ffn_w

    def act(width, dtype):
        return jax.ShapeDtypeStruct((nb, r, width), dtype)

    def act_spec(width):
        return pl.BlockSpec((1, tm, width), lambda b, t: (b, t, 0))

    four_shape = jax.ShapeDtypeStruct(n_four, BF16)
    four_spec = pl.BlockSpec((tm, 256), four_index[0])
    vec256 = _const_spec((1, 256))
    out_shapes = [act(d, F32), four_shape, four_shape, act(256, BF16), act(256, BF16), act(256, BF16),
                  act(512, F32), act(256, BF16), act(256, F32),
                  act(256, BF16), act(128, BF16), act(128, BF16)]
    out_specs = [act_spec(d), four_spec, four_spec, act_spec(256), act_spec(256), act_spec(256),
                 act_spec(512), act_spec(256), act_spec(256),
                 act_spec(256), act_spec(128), act_spec(128)]
    return pl.pallas_call(
        functools.partial(_inproj_kernel, rope=rope, fc=256),
        out_shape=out_shapes,
        grid=(nb, r // tm),
        in_specs=[
            pl.BlockSpec((1, tm, d), lambda b, t: (b, t, 0)),
            pl.BlockSpec((1, N_MOD, d), lambda b, t: (b, 0, 0)),
            _const_spec(w1.shape), _const_spec(w3.shape), _const_spec(w2.shape),
            _const_spec(lw["w_in"].shape),
            _const_spec((256, 256)), _const_spec((128, 128)),
            _const_spec((256, 256)), _const_spec((256, 256)),
            vec256, vec256, vec256, vec256,
            _const_spec((128, 256)), vec256,
            pl.BlockSpec((tm, 128), lambda b, t: (t, 0)),
            pl.BlockSpec((tm, 128), lambda b, t: (t, 0)),
        ],
        out_specs=out_specs,
        compiler_params=_cparams("parallel", "parallel"),
        name="ffn1_in_proj",
    )(x, mod, w1, w3, w2, lw["w_in"], consts["g256"], consts["g128"], consts["cc"], consts["sc"],
      lw["naq_g"], lw["nak_g"], lw["gq_g"], lw["gk_g"], lw["wg"], lw["bg"], cos_t, sin_t)


def _dft_kernel(c_ref, s_ref, uc_ref, us_ref, o_ref, acc_ref):
    k = pl.program_id(1)

    @pl.when(k == 0)
    def _():
        acc_ref[...] = jnp.zeros_like(acc_ref)

    acc_ref[...] += _dot(c_ref[...], uc_ref[...]) - _dot(s_ref[...], us_ref[...])

    @pl.when(k == pl.num_programs(1) - 1)
    def _():
        o_ref[...] = acc_ref[...].astype(o_ref.dtype)


def _dft(c_tab, s_tab, uc, us, tm, tk):
    l, n = uc.shape
    return pl.pallas_call(
        _dft_kernel,
        out_shape=jax.ShapeDtypeStruct((l, n), BF16),
        grid=(l // tm, l // tk),
        in_specs=[
            pl.BlockSpec((tm, tk), lambda i, k: (i, k)),
            pl.BlockSpec((tm, tk), lambda i, k: (i, k)),
            pl.BlockSpec((tk, n), lambda i, k: (k, 0)),
            pl.BlockSpec((tk, n), lambda i, k: (k, 0)),
        ],
        out_specs=pl.BlockSpec((tm, n), lambda i, k: (i, 0)),
        scratch_shapes=[pltpu.VMEM((tm, n), F32)],
        compiler_params=_cparams("parallel", "arbitrary"),
        name="position_dft",
    )(c_tab, s_tab, uc, us)


def _stack_heads(q, n_heads, width):
    lane = lax.broadcasted_iota(jnp.int32, q.shape, 1)
    zero = jnp.zeros_like(q)
    return jnp.concatenate([jnp.where(lane // width == h, q, zero) for h in range(n_heads)], axis=0)


def _unstack_heads(o, n_heads, width):
    m = o.shape[0] // n_heads
    lane = lax.broadcasted_iota(jnp.int32, (m, o.shape[1]), 1)
    out = jnp.zeros((m, o.shape[1]), F32)
    for h in range(n_heads):
        out = out + jnp.where(lane // width == h, o[h * m:(h + 1) * m, :], 0.0)
    return out


def _softmax_pv(score_blocks, value_blocks):
    m = score_blocks[0].max(axis=-1, keepdims=True)
    for s in score_blocks[1:]:
        m = jnp.maximum(m, s.max(axis=-1, keepdims=True))
    l = None
    o = None
    for s, v in zip(score_blocks, value_blocks):
        p = jnp.exp(s - m)
        ps = p.sum(axis=-1, keepdims=True)
        pv = _dot(p.astype(BF16), v)
        l = ps if l is None else l + ps
        o = pv if o is None else o + pv
    return o * (1.0 / l)


def _na_kernel(q_ref, k_ref, v_ref, kc_ref, vc_ref, tb_ref, o_ref, *, rows_per_step, n_rows):
    rb = pl.program_id(1)
    kc = kc_ref[0]
    vc = vc_ref[0]
    for i in range(rows_per_step):
        r = rb * rows_per_step + i
        rs = jnp.clip(r - NA_ROWS // 2, 0, n_rows - NA_ROWS)
        off = r - rs
        q = q_ref[0, i * GRID_W:(i + 1) * GRID_W, :]
        qs = _stack_heads(q, 4, HEAD_DIM)
        start = pl.multiple_of(rs * GRID_W, GRID_W)
        kw = k_ref[0, pl.ds(start, NA_ROWS * GRID_W), :]
        vw = v_ref[0, pl.ds(start, NA_ROWS * GRID_W), :]
        s_w = _dot_nt(qs, kw) + tb_ref[off]
        s_c = _dot_nt(qs, kc)
        o = _softmax_pv([s_w, s_c], [vw, vc])
        o_ref[0, i * GRID_W:(i + 1) * GRID_W, :] = _unstack_heads(o, 4, HEAD_DIM).astype(BF16)


def _na(q, k, v, kc, vc, tb):
    nb, s, w = q.shape
    n_rows = s // GRID_W
    rps = 8
    lc = kc.shape[1]
    return pl.pallas_call(
        functools.partial(_na_kernel, rows_per_step=rps, n_rows=n_rows),
        out_shape=jax.ShapeDtypeStruct((nb, s, w), BF16),
        grid=(nb, n_rows // rps),
        in_specs=[
            pl.BlockSpec((1, rps * GRID_W, w), lambda b, t: (b, t, 0)),
            pl.BlockSpec((1, s, w), lambda b, t: (b, 0, 0)),
            pl.BlockSpec((1, s, w), lambda b, t: (b, 0, 0)),
            pl.BlockSpec((1, lc, w), lambda b, t: (b, 0, 0)),
            pl.BlockSpec((1, lc, w), lambda b, t: (b, 0, 0)),
            _const_spec(tb.shape),
        ],
        out_specs=pl.BlockSpec((1, rps * GRID_W, w), lambda b, t: (b, t, 0)),
        compiler_params=_cparams("parallel", "parallel"),
        name="neighborhood_attn",
    )(q, k, v, kc, vc, tb)


def _gqa_scores_lhs(q):
    return jnp.concatenate([_stack_heads(q[:, 0:128], 2, HEAD_DIM),
                            _stack_heads(q[:, 128:256], 2, HEAD_DIM)], axis=0)


def _gqa_merge(o):
    m = o.shape[0] // 4
    return jnp.concatenate([_unstack_heads(o[0:2 * m], 2, HEAD_DIM),
                            _unstack_heads(o[2 * m:4 * m], 2, HEAD_DIM)], axis=1)


def _gqa_kernel(q_ref, k_ref, v_ref, kc_ref, vc_ref, o_ref, s_ref, p_ref, l_ref, *, kb, rb):
    qs = _gqa_scores_lhs(q_ref[0])
    n = qs.shape[0]
    m = jnp.full((n, 1), NEG, F32)
    acc = jnp.zeros((n, 128), F32)
    l_ref[...] = jnp.zeros_like(l_ref)
    blocks = [(k_ref, v_ref, j * kb, kb) for j in range(k_ref.shape[1] // kb)]
    blocks.append((kc_ref, vc_ref, 0, kc_ref.shape[1]))
    for kr, vr, k0, size in blocks:
        s = _dot_nt(qs, kr[0, k0:k0 + size, :])
        s_ref[:, 0:size] = s
        m_new = jnp.maximum(m, s.max(axis=-1, keepdims=True))
        alpha = jnp.exp(m - m_new)
        for r0 in range(0, n, rb):
            p = jnp.exp(s_ref[r0:r0 + rb, 0:size] - m_new[r0:r0 + rb])
            part = p[:, 0:128]
            for j in range(1, size // 128):
                part = part + p[:, j * 128:(j + 1) * 128]
            l_ref[r0:r0 + rb, :] = l_ref[r0:r0 + rb, :] * alpha[r0:r0 + rb] + part
            p_ref[r0:r0 + rb, 0:size] = p.astype(BF16)
        acc = acc * alpha + _dot(p_ref[:, 0:size], vr[0, k0:k0 + size, :])
        m = m_new
    l = l_ref[...].sum(axis=-1, keepdims=True)
    o_ref[0] = _gqa_merge(acc * (1.0 / l)).astype(BF16)


def _gqa(q, k, v, kc, vc, tq, kb=512):
    nb, s, _ = q.shape
    lc = kc.shape[1]
    return pl.pallas_call(
        functools.partial(_gqa_kernel, kb=kb, rb=16),
        out_shape=jax.ShapeDtypeStruct((nb, s, 256), BF16),
        grid=(nb, s // tq),
        in_specs=[
            pl.BlockSpec((1, tq, 256), lambda b, t: (b, t, 0)),
            pl.BlockSpec((1, s, 128), lambda b, t: (b, 0, 0)),
            pl.BlockSpec((1, s, 128), lambda b, t: (b, 0, 0)),
            pl.BlockSpec((1, lc, 128), lambda b, t: (b, 0, 0)),
            pl.BlockSpec((1, lc, 128), lambda b, t: (b, 0, 0)),
        ],
        out_specs=pl.BlockSpec((1, tq, 256), lambda b, t: (b, t, 0)),
        scratch_shapes=[pltpu.VMEM((4 * tq, kb), F32), pltpu.VMEM((4 * tq, kb), BF16),
                        pltpu.VMEM((4 * tq, 128), F32)],
        compiler_params=_cparams("parallel", "parallel"),
        name="gqa_attn",
    )(q, k, v, kc, vc)


def _ctx_attn_kernel(naq_ref, nak_ref, nav_ref, q_ref, k_ref, v_ref, bz_ref, dz_ref):
    qs = _stack_heads(naq_ref[0], 4, HEAD_DIM)
    o = _softmax_pv([_dot_nt(qs, nak_ref[0])], [nav_ref[0]])
    bz_ref[0] = _unstack_heads(o, 4, HEAD_DIM).astype(BF16)
    qg = _gqa_scores_lhs(q_ref[0])
    og = _softmax_pv([_dot_nt(qg, k_ref[0])], [v_ref[0]])
    dz_ref[0] = _gqa_merge(og).astype(BF16)


def _ctx_attn(naq, nak, nav, q, k, v):
    nb, lc, _ = naq.shape

    def spec(w):
        return pl.BlockSpec((1, lc, w), lambda b: (b, 0, 0))

    out = jax.ShapeDtypeStruct((nb, lc, 256), BF16)
    return pl.pallas_call(
        _ctx_attn_kernel,
        out_shape=[out, out],
        grid=(nb,),
        in_specs=[spec(256), spec(256), spec(256), spec(256), spec(128), spec(128)],
        out_specs=[spec(256), spec(256)],
        compiler_params=_cparams("parallel"),
        name="context_attn",
    )(naq, nak, nav, q, k, v)


def _gla_kernel(qf_ref, vf_ref, qb_ref, vb_ref, z_ref, mask_ref, s0_ref, of_ref, ob_ref, sfin_ref,
                st_ref, *, chunks):
    p = pl.program_id(1)
    c = GLA_CHUNK

    @pl.when(p == 0)
    def _():
        st_ref[...] = s0_ref[0]

    row = lax.broadcasted_iota(jnp.int32, (256, 128), 0)
    lane = lax.broadcasted_iota(jnp.int32, (256, 128), 1)
    bd_mask = (row // HEAD_DIM) == (lane // GLA_DK)
    row4 = lax.broadcasted_iota(jnp.int32, (4 * c, 128), 0)
    lane4 = lax.broadcasted_iota(jnp.int32, (4 * c, 128), 1)
    head_mask = (row4 // c) == (lane4 // GLA_DK)
    n_lev = len(GLA_LEVELS)
    zq = jnp.zeros((4 * c, 128), BF16)
    zk = jnp.zeros((c, 128), BF16)

    units = []
    for ci in range(chunks):
        units.append((0, qf_ref, vf_ref, of_ref, ci * c, 256))
        units.append((1, qb_ref, vb_ref, ob_ref, (chunks - 1 - ci) * c, 384))

    loaded = []
    for d, in_ref, v_ref, o_ref, r0, a0 in units:
        q = in_ref[0, r0:r0 + c, 0:128]
        k = in_ref[0, r0:r0 + c, 128:256]
        a = in_ref[0, r0:r0 + c, a0:a0 + 128]
        v = v_ref[0, r0:r0 + c, :]
        a_hi = a.astype(BF16)
        a_lo = (a - a_hi.astype(F32)).astype(BF16)
        e2 = _dot(z_ref[d], jnp.concatenate([a_hi, a_lo], axis=1))
        x = jnp.exp(e2[:, 0:128] + e2[:, 128:256])
        loaded.append((q, k, v, x))

    staged = []
    for (d, _, _, _, _, _), (q, k, v, x) in zip(units, loaded):

        def level_operands(lev):
            if lev < n_lev:
                xl = x[(2 + lev) * c:(3 + lev) * c]
                ql, kl = q * xl, k * xl
            else:
                ql, kl = q, k
            qs = jnp.where(head_mask, jnp.concatenate([ql, ql, ql, ql], axis=0), 0.0)
            return qs.astype(BF16), kl.astype(BF16)

        scores = jnp.zeros((4 * c, 2 * c), F32)
        for pair in range((n_lev + 2) // 2):
            qa, ka = level_operands(2 * pair)
            qb, kb = level_operands(2 * pair + 1) if 2 * pair + 1 <= n_lev else (zq, zk)
            lhs = jnp.concatenate([qa, qb], axis=1)
            rhs = jnp.concatenate([jnp.concatenate([ka, zk], axis=1),
                                   jnp.concatenate([zk, kb], axis=1)], axis=0)
            scores = scores + _dot_nt(lhs, rhs) * mask_ref[d, pair]
        o_all = _dot(scores.astype(BF16), jnp.concatenate([v, v], axis=0))
        o_intra = _unstack_heads(o_all, 4, HEAD_DIM)
        qe = (q * x[0:c]).astype(BF16)
        ke = (k * x[c:2 * c]).astype(BF16)
        upd = jnp.where(bd_mask, _dot_tn(v, ke), 0.0)
        decay = x[8 * c:8 * c + 1, :]
        staged.append((o_intra, qe, upd, decay))

    st = [st_ref[0], st_ref[1]]
    for (d, _, _, o_ref, r0, _), (o_intra, qe, upd, decay) in zip(units, staged):
        o_ref[0, r0:r0 + c, :] = o_intra + _dot_nt(qe, st[d].astype(BF16))
        st[d] = st[d] * decay + upd
    st_ref[0] = st[0]
    st_ref[1] = st[1]

    @pl.when(p == pl.num_programs(1) - 1)
    def _():
        sfin_ref[0] = st_ref[...]


def _gla(qka, v, s0, z_tab, masks):
    nb, r, _ = qka.shape
    chunks = 4
    blk = chunks * GLA_CHUNK
    nblk = r // blk
    out = jax.ShapeDtypeStruct((nb, r, 256), F32)
    return pl.pallas_call(
        functools.partial(_gla_kernel, chunks=chunks),
        out_shape=[out, out, jax.ShapeDtypeStruct((nb, 2, 256, 128), F32)],
        grid=(nb, nblk),
        in_specs=[
            pl.BlockSpec((1, blk, 512), lambda b, p: (b, p, 0)),
            pl.BlockSpec((1, blk, 256), lambda b, p: (b, p, 0)),
            pl.BlockSpec((1, blk, 512), lambda b, p: (b, nblk - 1 - p, 0)),
            pl.BlockSpec((1, blk, 256), lambda b, p: (b, nblk - 1 - p, 0)),
            _const_spec(z_tab.shape),
            _const_spec(masks.shape),
            pl.BlockSpec((1, 2, 256, 128), lambda b, p: (b, 0, 0, 0)),
        ],
        out_specs=[
            pl.BlockSpec((1, blk, 256), lambda b, p: (b, p, 0)),
            pl.BlockSpec((1, blk, 256), lambda b, p: (b, nblk - 1 - p, 0)),
            pl.BlockSpec((1, 2, 256, 128), lambda b, p: (b, 0, 0, 0)),
        ],
        scratch_shapes=[pltpu.VMEM((2, 256, 128), F32)],
        compiler_params=_cparams("parallel", "arbitrary"),
        name="gla_scan",
    )(qka, v, qka, v, z_tab, masks, s0)


def _outproj_kernel(x_ref, mod_ref, a_ref, b_ref, of_ref, ob_ref, r_ref, d_ref, w_ref, g256_ref,
                    gn_ref, o_ref):
    o = of_ref[0] + ob_ref[0]
    cx = (_head_rms(o, g256_ref[...], gn_ref[...]) * _silu(r_ref[0])).astype(BF16)
    acc = _dot(a_ref[...], w_ref[0:256, :])
    acc = acc + _dot(b_ref[0], w_ref[256:512, :])
    acc = acc + _dot(cx, w_ref[512:768, :])
    acc = acc + _dot(d_ref[0], w_ref[768:1024, :])
    o_ref[0] = x_ref[0] + mod_ref[0, 5:6, :] * acc


def _outproj(x, mod, a, a_index, b, o_f, o_b, r, dd, w_out, g256, gn, tm):
    nb, rows, d = x.shape

    def act_spec(width):
        return pl.BlockSpec((1, tm, width), lambda bb, t: (bb, t, 0))

    return pl.pallas_call(
        _outproj_kernel,
        out_shape=jax.ShapeDtypeStruct(x.shape, F32),
        grid=(nb, rows // tm),
        in_specs=[
            act_spec(d),
            pl.BlockSpec((1, N_MOD, d), lambda bb, t: (bb, 0, 0)),
            pl.BlockSpec((tm, 256), a_index),
            act_spec(256), act_spec(256), act_spec(256), act_spec(256), act_spec(256),
            _const_spec(w_out.shape), _const_spec((256, 256)), _const_spec((1, 256)),
        ],
        out_specs=act_spec(d),
        compiler_params=_cparams("parallel", "parallel"),
        name="out_proj",
    )(x, mod, a, b, o_f, o_b, r, dd, w_out, g256, gn)


def _ffn_kernel(x_ref, mod_ref, w1_ref, w3_ref, w2_ref, o_ref, *, row0, fc):
    x = x_ref[0]
    hb = _norm_mod(x, mod_ref[0, row0:row0 + 1, :], mod_ref[0, row0 + 1:row0 + 2, :]).astype(BF16)
    o_ref[0] = x + (0.5 * mod_ref[0, row0 + 2:row0 + 3, :]) * _swiglu_acc(hb, w1_ref, w3_ref, w2_ref, fc)


def _ffn(x, mod, ffn_w, row0, tm):
    nb, r, d = x.shape
    w1, w3, w2 = ffn_w
    return pl.pallas_call(
        functools.partial(_ffn_kernel, row0=row0, fc=256),
        out_shape=jax.ShapeDtypeStruct(x.shape, F32),
        grid=(nb, r // tm),
        in_specs=[
            pl.BlockSpec((1, tm, d), lambda b, t: (b, t, 0)),
            pl.BlockSpec((1, N_MOD, d), lambda b, t: (b, 0, 0)),
            _const_spec(w1.shape), _const_spec(w3.shape), _const_spec(w2.shape),
        ],
        out_specs=pl.BlockSpec((1, tm, d), lambda b, t: (b, t, 0)),
        compiler_params=_cparams("parallel", "parallel"),
        name="ffn_halfstep",
    )(x, mod, w1, w3, w2)


def _dft_tables(l):
    r = 1 << (int(math.log2(l)) // 2)
    k = jnp.arange(l, dtype=jnp.int32)[None, :]

    def trig(n):
        ang = ((n[:, None] * k) % l).astype(F32) * (2.0 * math.pi / l)
        return jnp.cos(ang), jnp.sin(ang)

    ch, sh = trig(jnp.arange(l // r, dtype=jnp.int32) * r)
    cl, sl = trig(jnp.arange(r, dtype=jnp.int32))
    scale = l ** -0.5
    cos = (ch[:, None, :] * cl[None, :, :] - sh[:, None, :] * sl[None, :, :]).reshape(l, l)
    sin = (sh[:, None, :] * cl[None, :, :] + ch[:, None, :] * sl[None, :, :]).reshape(l, l)
    return (cos * scale).astype(BF16), (sin * scale).astype(BF16)


def _channel_dft_tables():
    n = jnp.arange(256, dtype=jnp.int32)
    same = (n[:, None] // HEAD_DIM) == (n[None, :] // HEAD_DIM)
    ang = (((n[:, None] % HEAD_DIM) * (n[None, :] % HEAD_DIM)) % HEAD_DIM).astype(F32) * (2.0 * math.pi / HEAD_DIM)
    scale = HEAD_DIM ** -0.5
    cc = jnp.where(same, jnp.cos(ang) * scale, 0.0).astype(BF16)
    sc = jnp.where(same, jnp.sin(ang) * scale, 0.0).astype(BF16)
    return cc, sc


def _group_matrix(width):
    n = jnp.arange(width, dtype=jnp.int32)
    return ((n[:, None] // HEAD_DIM) == (n[None, :] // HEAD_DIM)).astype(BF16)


def _rope_tables(n_tokens):
    t = jnp.arange(n_tokens, dtype=jnp.int32)
    row = (t // GRID_W).astype(F32)
    col = (t % GRID_W).astype(F32)
    n_freq = HEAD_DIM // 4
    inv_freq = ROPE_THETA ** (-jnp.arange(n_freq, dtype=F32) / n_freq)
    ang = jnp.concatenate([row[:, None] * inv_freq, col[:, None] * inv_freq], axis=-1)
    cos, sin = jnp.cos(ang), jnp.sin(ang)
    cos_t = jnp.concatenate([cos, cos, cos, cos], axis=-1)
    sin_t = jnp.concatenate([-sin, sin, -sin, sin], axis=-1)
    return cos_t, sin_t


def _gla_tables():
    c = GLA_CHUNK
    i = jnp.arange(c, dtype=jnp.int32)[:, None]
    t = jnp.arange(c, dtype=jnp.int32)[None, :]
    ones_row = jnp.ones((16, c), jnp.bool_)

    def tables(forward):
        z = [t <= i, t > i] if forward else [t >= i, t < i]
        m = []
        for s in GLA_LEVELS:
            pos = i % (2 * s)
            base = i - pos
            bound = base + s - 1
            right = pos >= s
            if forward:
                zq = right & (t > bound) & (t <= i)
                zk = (~right) & (t > i) & (t <= bound)
            else:
                zq = (~right) & (t >= i) & (t <= bound)
                zk = right & (t > bound) & (t < i)
            z.append(zq | zk)
            same_parent = (i // (2 * s)) == (t // (2 * s))
            t_right = (t % (2 * s)) >= s
            if forward:
                m.append(same_parent & right & (~t_right))
            else:
                m.append(same_parent & (~right) & t_right)
        m.append(i == t)
        z.append(ones_row)
        zt = jnp.concatenate(z, axis=0).astype(BF16)
        m.append(jnp.zeros_like(m[0]))
        mt = jnp.stack([jnp.tile(jnp.concatenate([m[2 * p], m[2 * p + 1]], axis=1), (4, 1))
                        for p in range(len(m) // 2)]).astype(F32)
        return zt, mt

    zf, mf = tables(True)
    zb, mb = tables(False)
    return jnp.stack([zf, zb]), jnp.stack([mf, mb])


def _na_bias_table(rpb):
    c = jnp.arange(GRID_W)[:, None, None]
    kc = jnp.arange(GRID_W)[None, :, None]
    m = jnp.arange(2 * NA_COLS - 1)[None, None, :]
    cs = jnp.clip(c - NA_COLS // 2, 0, GRID_W - NA_COLS)
    valid = (kc >= cs) & (kc < cs + NA_COLS)
    pick = (valid & (kc - c + NA_COLS - 1 == m)).astype(F32)
    by_row = jnp.einsum('hrm,ckm->rhck', rpb.astype(F32), pick, precision=lax.Precision.HIGHEST)
    by_row = jnp.where(valid[None, None, :, :, 0], by_row, NEG)
    blocks = [jnp.concatenate([by_row[j - off + NA_ROWS - 1] for j in range(NA_ROWS)], axis=-1)
              for off in range(NA_ROWS)]
    return jnp.stack(blocks).reshape(NA_ROWS, 4 * GRID_W, NA_ROWS * GRID_W)


_GQA_PERM = (0, 2, 1, 3)


def _layer_weights(l, w_in, na_q_norm, na_k_norm, gla_w_gate_f, gla_b_gate_f, gla_w_gate_b, gla_b_gate_b,
                   gla_norm, gqa_q_norm, gqa_k_norm, w_out):
    wi = w_in[l]
    d = wi.shape[0]
    gq = wi[:, 1824:2080].reshape(d, 4, HEAD_DIM)[:, _GQA_PERM, :].reshape(d, 256)
    w_in_r = jnp.concatenate([wi[:, 0:1792], gq, wi[:, 2080:2336], wi[:, 1792:1824],
                              jnp.zeros((d, 96), F32)], axis=1).astype(BF16)
    wg = jnp.zeros((128, 256), F32)
    wg = wg.at[0:16, 0:128].set(gla_w_gate_f[l]).at[16:32, 128:256].set(gla_w_gate_b[l]).astype(BF16)
    bg = jnp.concatenate([gla_b_gate_f[l], gla_b_gate_b[l]])[None, :]
    wo = w_out[l]
    wo_d = wo[768:1024].reshape(4, HEAD_DIM, -1)[_GQA_PERM, :, :].reshape(256, -1)
    w_out_r = jnp.concatenate([wo[0:768], wo_d], axis=0).astype(BF16)

    def tile4(g):
        return jnp.tile(g, 4)[None, :]

    return {
        "w_in": w_in_r, "wg": wg, "bg": bg, "w_out": w_out_r,
        "naq_g": tile4(na_q_norm[l]), "nak_g": tile4(na_k_norm[l]),
        "gq_g": tile4(gqa_q_norm[l]), "gk_g": tile4(gqa_k_norm[l]),
        "gn": tile4(gla_norm[l]),
    }


def kernel(x, c, ctx, c_ctx, w_mod, b_mod, ffn1_w1, ffn1_w3, ffn1_w2, w_in, na_q_norm, na_k_norm, na_rpb,
           gla_w_gate_f, gla_b_gate_f, gla_w_gate_b, gla_b_gate_b, gla_norm, gqa_q_norm, gqa_k_norm, w_out,
           ffn2_w1, ffn2_w3, ffn2_w2):
    nb, s, d = x.shape
    lc = ctx.shape[1]
    depth = w_mod.shape[0]
    tm_x = 512
    tm_z = 256

    cc = jnp.zeros((16, d), F32).at[0:nb].set(c).at[nb].set(c_ctx)
    mod_all = _modulation(cc, w_mod, b_mod)

    consts = {"g256": _group_matrix(256), "g128": _group_matrix(128)}
    consts["cc"], consts["sc"] = _channel_dft_tables()
    cos_t, sin_t = _rope_tables(s)
    dft_x = _dft_tables(s)
    dft_z = _dft_tables(lc)
    z_tab, gla_masks = _gla_tables()
    s_zero = jnp.zeros((nb, 2, 256, 128), F32)

    z = ctx.reshape(1, nb * lc, d)
    four_x = (lambda b, t: (t, b), (s, nb * 256))
    four_z = (lambda b, t: (0, t), (lc, nb * 256))

    for l in range(depth):
        last = l == depth - 1
        mod = mod_all[l, 0:nb + 1].reshape(nb + 1, N_MOD, d)
        mod_x, mod_z = mod[0:nb], mod[nb:nb + 1]
        lw = _layer_weights(l, w_in, na_q_norm, na_k_norm, gla_w_gate_f, gla_b_gate_f, gla_w_gate_b,
                            gla_b_gate_b, gla_norm, gqa_q_norm, gqa_k_norm, w_out)
        f1 = (ffn1_w1[l].astype(BF16), ffn1_w3[l].astype(BF16), ffn1_w2[l].astype(BF16))
        f2 = (ffn2_w1[l].astype(BF16), ffn2_w3[l].astype(BF16), ffn2_w2[l].astype(BF16))
        tb = _na_bias_table(na_rpb[l])

        px = _ffn_inproj(x, mod_x, f1, lw, consts, cos_t, sin_t, True, tm_x, four_x)
        pz = _ffn_inproj(z, mod_z, f1, lw, consts, cos_t, sin_t, False, tm_z, four_z)
        (x, ucx, usx, naq_x, nak_x, nav_x, qka_x, gv_x, gr_x, q_x, k_x, v_x) = px
        z = pz[0]
        (ucz, usz, naq_z, nak_z, nav_z, qka_z, gv_z, gr_z, q_z, k_z, v_z) = [
            t if t.ndim == 2 else t.reshape(nb, lc, t.shape[-1]) for t in pz[1:]]

        a_x = _dft(dft_x[0], dft_x[1], ucx, usx, tm=1024, tk=512)
        b_x = _na(naq_x, nak_x, nav_x, nak_z, nav_z, tb)
        ofz, obz, s_ctx = _gla(qka_z, gv_z, s_zero, z_tab, gla_masks)
        ofx, obx, _ = _gla(qka_x, gv_x, s_ctx, z_tab, gla_masks)
        d_x = _gqa(q_x, k_x, v_x, k_z, v_z, tq=256, kb=1024)

        x = _outproj(x, mod_x, a_x, four_x[0], b_x, ofx, obx, gr_x, d_x, lw["w_out"], consts["g256"],
                     lw["gn"], tm_x)
        x = _ffn(x, mod_x, f2, row0=6, tm=tm_x)

        if not last:
            a_z = _dft(dft_z[0], dft_z[1], ucz, usz, tm=lc, tk=lc)
            b_z, d_z = _ctx_attn(naq_z, nak_z, nav_z, q_z, k_z, v_z)
            z = _outproj(z, mod_z, a_z, four_z[0], b_z.reshape(1, nb * lc, 256),
                         ofz.reshape(1, nb * lc, 256), obz.reshape(1, nb * lc, 256),
                         gr_z.reshape(1, nb * lc, 256), d_z.reshape(1, nb * lc, 256),
                         lw["w_out"], consts["g256"], lw["gn"], tm_z)
            z = _ffn(z, mod_z, f2, row0=6, tm=tm_z)
    return x
```

```python
import functools
import math

import jax
import jax.numpy as jnp
from jax import lax
from jax.experimental import pallas as pl
from jax.experimental.pallas import tpu as pltpu

F32 = jnp.float32
BF16 = jnp.bfloat16

HEAD_DIM = 64
GRID_W = 64
NA_ROWS = 8
NA_COLS = 16
GLA_DK = 32
GLA_TAU = 16.0
GLA_CHUNK = 64
GLA_LEVELS = (32, 16, 8, 4, 2, 1)
ROPE_THETA = 10000.0
EPS = 1e-6
N_MOD = 9
NEG = -1e30

VMEM_LIMIT = 56 * 1024 * 1024


def _cparams(*sem):
    return pltpu.CompilerParams(dimension_semantics=sem, vmem_limit_bytes=VMEM_LIMIT)


def _dot(a, b):
    return jnp.dot(a, b, preferred_element_type=F32)


def _dot_nt(a, b):
    return lax.dot_general(a, b, (((1,), (1,)), ((), ())), preferred_element_type=F32)


def _dot_tn(a, b):
    return lax.dot_general(a, b, (((0,), (0,)), ((), ())), preferred_element_type=F32)


def _sigmoid(x):
    return 1.0 / (1.0 + jnp.exp(-x))


def _silu(x):
    return x * _sigmoid(x)


def _log_sigmoid(x):
    return jnp.minimum(x, 0.0) - jnp.log(1.0 + jnp.exp(-jnp.abs(x)))


def _const_spec(shape):
    nd = len(shape)
    return pl.BlockSpec(shape, lambda *_: (0,) * nd, pipeline_mode=pl.Buffered(1))


def _mod_kernel(c_ref, w_ref, b_ref, o_ref):
    h = _silu(c_ref[...]).astype(BF16)
    o_ref[0] = _dot(h, w_ref[0].astype(BF16)) + b_ref[0]


def _modulation(cc, w_mod, b_mod):
    depth, d, n = w_mod.shape
    rows = cc.shape[0]
    tn = 1024
    return pl.pallas_call(
        _mod_kernel,
        out_shape=jax.ShapeDtypeStruct((depth, rows, n), F32),
        grid=(depth, n // tn),
        in_specs=[
            pl.BlockSpec((rows, d), lambda l, j: (0, 0)),
            pl.BlockSpec((1, d, tn), lambda l, j: (l, 0, j)),
            pl.BlockSpec((1, 1, tn), lambda l, j: (l, 0, j)),
        ],
        out_specs=pl.BlockSpec((1, rows, tn), lambda l, j: (l, 0, j)),
        compiler_params=_cparams("parallel", "parallel"),
        name="adaln_mod",
    )(cc, w_mod, b_mod.reshape(depth, 1, n))


def _norm_mod(x, shift, scale):
    h = x * lax.rsqrt(jnp.mean(x * x, axis=-1, keepdims=True) + EPS)
    return h * (1.0 + scale) + shift


def _swiglu_acc(hb, w1_ref, w3_ref, w2_ref, fc):
    d_ff = w1_ref.shape[1]
    acc = jnp.zeros((hb.shape[0], w2_ref.shape[1]), F32)
    for f0 in range(0, d_ff, fc):
        a = _dot(hb, w1_ref[:, f0:f0 + fc])
        b = _dot(hb, w3_ref[:, f0:f0 + fc])
        g = (_silu(a) * b).astype(BF16)
        acc = acc + _dot(g, w2_ref[f0:f0 + fc, :])
    return acc


def _group_sum(xsq, gmat):
    hi = xsq.astype(BF16)
    lo = (xsq - hi.astype(F32)).astype(BF16)
    return _dot(hi, gmat) + _dot(lo, gmat)


def _head_rms(x, gmat, gain):
    ms = _group_sum(x * x, gmat) * (1.0 / HEAD_DIM)
    return x * lax.rsqrt(ms + EPS) * gain


def _rope128(x, cos, sin_signed):
    lane = lax.broadcasted_iota(jnp.int32, x.shape, 1)
    first_half = (lane % HEAD_DIM) < (HEAD_DIM // 2)
    partner = jnp.where(first_half, pltpu.roll(x, 96, 1), pltpu.roll(x, 32, 1))
    return x * cos + partner * sin_signed


def _inproj_kernel(x_ref, mod_ref, w1_ref, w3_ref, w2_ref, w_ref, g256_ref, g128_ref, cc_ref, sc_ref,
                   naq_g_ref, nak_g_ref, gq_g_ref, gk_g_ref, wg_ref, bg_ref, cos_ref, sin_ref,
                   x1_ref, uc_ref, us_ref, naq_ref, nak_ref, nav_ref, qka_ref, gv_ref, gr_ref,
                   q_ref, k_ref, v_ref, four_ref, *, rope, fc):
    x0 = x_ref[0]
    hb0 = _norm_mod(x0, mod_ref[0, 0:1, :], mod_ref[0, 1:2, :]).astype(BF16)
    x = x0 + (0.5 * mod_ref[0, 2:3, :]) * _swiglu_acc(hb0, w1_ref, w3_ref, w2_ref, fc)
    x1_ref[0] = x
    shift = mod_ref[0, 3:4, :]
    scale = mod_ref[0, 4:5, :]
    hb = _norm_mod(x, shift, scale).astype(BF16)
    h = _dot(hb, w_ref[...])
    g256 = g256_ref[...]
    g128 = g128_ref[...]
    ub = h[:, 0:256].astype(BF16)
    half = four_ref.shape[1] // 2
    for w, (out_ref, tab_ref) in enumerate(((uc_ref, cc_ref), (us_ref, sc_ref))):
        u = _dot(ub, tab_ref[...])
        for j in range(2):
            s = 2 * w + j
            four_ref[s] = u[:, j * 128:(j + 1) * 128]
            out_ref[0, :, j * 128:(j + 1) * 128] = four_ref[s, pl.ds(0, half, stride=2), :].astype(BF16)
            out_ref[1, :, j * 128:(j + 1) * 128] = four_ref[s, pl.ds(1, half, stride=2), :].astype(BF16)
    naq_ref[0] = (_head_rms(h[:, 256:512], g256, naq_g_ref[...]) * HEAD_DIM ** -0.5).astype(BF16)
    nak_ref[0] = _head_rms(h[:, 512:768], g256, nak_g_ref[...]).astype(BF16)
    nav_ref[0] = h[:, 768:1024].astype(BF16)
    gates = _dot(h[:, 2304:2432].astype(BF16), wg_ref[...]) + bg_ref[...]
    log_a = _log_sigmoid(gates) / GLA_TAU
    qka_ref[0, :, 0:128] = h[:, 1024:1152] * GLA_DK ** -0.5
    qka_ref[0, :, 128:256] = h[:, 1152:1280]
    qka_ref[0, :, 256:512] = log_a
    gv_ref[0] = h[:, 1280:1536].astype(BF16)
    gr_ref[0] = _silu(h[:, 1536:1792]).astype(BF16)
    q = _head_rms(h[:, 1792:2048], g256, gq_g_ref[...])
    k = _head_rms(h[:, 2048:2176], g128, gk_g_ref[:, 0:128])
    if rope:
        cos = cos_ref[...]
        sin = sin_ref[...]
        q = jnp.concatenate([_rope128(q[:, 0:128], cos, sin), _rope128(q[:, 128:256], cos, sin)], axis=1)
        k = _rope128(k, cos, sin)
    q_ref[0] = (q * HEAD_DIM ** -0.5).astype(BF16)
    k_ref[0] = k.astype(BF16)
    v_ref[0] = h[:, 2176:2304].astype(BF16)


def _ffn_inproj(x, mod, ffn_w, lw, consts, cos_t, sin_t, rope, tm, four_index):
    nb, r, d = x.shape
    n_four = four_index[1]
    w1, w3, w2 = ffn_w

    def act(width, dtype):
        return jax.ShapeDtypeStruct((nb, r, width), dtype)

    def act_spec(width):
        return pl.BlockSpec((1, tm, width), lambda b, t: (b, t, 0))

    four_shape = jax.ShapeDtypeStruct(n_four, BF16)
    four_spec = pl.BlockSpec((2, tm // 2, 256), four_index[0])
    vec256 = _const_spec((1, 256))
    out_shapes = [act(d, F32), four_shape, four_shape, act(256, BF16), act(256, BF16), act(256, BF16),
                  act(512, F32), act(256, BF16), act(256, BF16),
                  act(256, BF16), act(128, BF16), act(128, BF16)]
    out_specs = [act_spec(d), four_spec, four_spec, act_spec(256), act_spec(256), act_spec(256),
                 act_spec(512), act_spec(256), act_spec(256),
                 act_spec(256), act_spec(128), act_spec(128)]
    return pl.pallas_call(
        functools.partial(_inproj_kernel, rope=rope, fc=256),
        out_shape=out_shapes,
        grid=(nb, r // tm),
        in_specs=[
            pl.BlockSpec((1, tm, d), lambda b, t: (b, t, 0)),
            pl.BlockSpec((1, N_MOD, d), lambda b, t: (b, 0, 0)),
            _const_spec(w1.shape), _const_spec(w3.shape), _const_spec(w2.shape),
            _const_spec(lw["w_in"].shape),
            _const_spec((256, 256)), _const_spec((128, 128)),
            _const_spec((256, 256)), _const_spec((256, 256)),
            vec256, vec256, vec256, vec256,
            _const_spec((128, 256)), vec256,
            pl.BlockSpec((tm, 128), lambda b, t: (t, 0)),
            pl.BlockSpec((tm, 128), lambda b, t: (t, 0)),
        ],
        out_specs=out_specs,
        scratch_shapes=[pltpu.VMEM((4, tm, 128), F32)],
        compiler_params=_cparams("parallel", "parallel"),
        name="ffn1_in_proj",
    )(x, mod, w1, w3, w2, lw["w_in"], consts["g256"], consts["g128"], consts["cc"], consts["sc"],
      lw["naq_g"], lw["nak_g"], lw["gq_g"], lw["gk_g"], lw["wg"], lw["bg"], cos_t, sin_t)


def _dft_kernel(c_ref, s_ref, uc_ref, us_ref, lo_ref, hi_ref):
    h = uc_ref.shape[1]
    a_e = _dot(c_ref[:, 0:h], uc_ref[0]) - _dot(s_ref[:, 0:h], us_ref[0])
    a_o = _dot(c_ref[:, h:2 * h], uc_ref[1]) - _dot(s_ref[:, h:2 * h], us_ref[1])
    lo_ref[...] = (a_e + a_o).astype(lo_ref.dtype)
    hi_ref[...] = (a_e - a_o).astype(hi_ref.dtype)


def _dft(c_tab, s_tab, uc, us, tm):
    _, h, n = uc.shape
    half = jax.ShapeDtypeStruct((h, n), BF16)
    lo, hi = pl.pallas_call(
        _dft_kernel,
        out_shape=[half, half],
        grid=(h // tm,),
        in_specs=[
            pl.BlockSpec((tm, 2 * h), lambda i: (i, 0)),
            pl.BlockSpec((tm, 2 * h), lambda i: (i, 0)),
            _const_spec(uc.shape),
            _const_spec(us.shape),
        ],
        out_specs=[pl.BlockSpec((tm, n), lambda i: (i, 0)), pl.BlockSpec((tm, n), lambda i: (i, 0))],
        compiler_params=_cparams("parallel"),
        name="position_dft",
    )(c_tab, s_tab, uc, us)
    return jnp.concatenate([lo, hi], axis=0)


def _stack_heads(q, n_heads, width):
    lane = lax.broadcasted_iota(jnp.int32, q.shape, 1)
    zero = jnp.zeros_like(q)
    return jnp.concatenate([jnp.where(lane // width == h, q, zero) for h in range(n_heads)], axis=0)


def _unstack_heads(o, n_heads, width):
    m = o.shape[0] // n_heads
    lane = lax.broadcasted_iota(jnp.int32, (m, o.shape[1]), 1)
    out = jnp.zeros((m, o.shape[1]), F32)
    for h in range(n_heads):
        out = out + jnp.where(lane // width == h, o[h * m:(h + 1) * m, :], 0.0)
    return out


def _softmax_pv(score_blocks, value_blocks):
    m = score_blocks[0].max(axis=-1, keepdims=True)
    for s in score_blocks[1:]:
        m = jnp.maximum(m, s.max(axis=-1, keepdims=True))
    l = None
    o = None
    for s, v in zip(score_blocks, value_blocks):
        p = jnp.exp(s - m)
        ps = p.sum(axis=-1, keepdims=True)
        pv = _dot(p.astype(BF16), v)
        l = ps if l is None else l + ps
        o = pv if o is None else o + pv
    return o * (1.0 / l)


def _na_kernel(q_ref, k_ref, v_ref, kc_ref, vc_ref, tb_ref, o_ref, *, rows_per_step, n_rows):
    rb = pl.program_id(1)
    kc = kc_ref[0]
    vc = vc_ref[0]
    for i in range(rows_per_step):
        r = rb * rows_per_step + i
        rs = jnp.clip(r - NA_ROWS // 2, 0, n_rows - NA_ROWS)
        off = r - rs
        q = q_ref[0, i * GRID_W:(i + 1) * GRID_W, :]
        qs = _stack_heads(q, 4, HEAD_DIM)
        start = pl.multiple_of(rs * GRID_W, GRID_W)
        kw = k_ref[0, pl.ds(start, NA_ROWS * GRID_W), :]
        vw = v_ref[0, pl.ds(start, NA_ROWS * GRID_W), :]
        s_w = _dot_nt(qs, kw) + tb_ref[off]
        s_c = _dot_nt(qs, kc)
        o = _softmax_pv([s_w, s_c], [vw, vc])
        o_ref[0, i * GRID_W:(i + 1) * GRID_W, :] = _unstack_heads(o, 4, HEAD_DIM).astype(BF16)


def _na(q, k, v, kc, vc, tb):
    nb, s, w = q.shape
    n_rows = s // GRID_W
    rps = 8
    lc = kc.shape[1]
    return pl.pallas_call(
        functools.partial(_na_kernel, rows_per_step=rps, n_rows=n_rows),
        out_shape=jax.ShapeDtypeStruct((nb, s, w), BF16),
        grid=(nb, n_rows // rps),
        in_specs=[
            pl.BlockSpec((1, rps * GRID_W, w), lambda b, t: (b, t, 0)),
            pl.BlockSpec((1, s, w), lambda b, t: (b, 0, 0)),
            pl.BlockSpec((1, s, w), lambda b, t: (b, 0, 0)),
            pl.BlockSpec((1, lc, w), lambda b, t: (b, 0, 0)),
            pl.BlockSpec((1, lc, w), lambda b, t: (b, 0, 0)),
            _const_spec(tb.shape),
        ],
        out_specs=pl.BlockSpec((1, rps * GRID_W, w), lambda b, t: (b, t, 0)),
        compiler_params=_cparams("parallel", "parallel"),
        name="neighborhood_attn",
    )(q, k, v, kc, vc, tb)


def _gqa_scores_lhs(q):
    return jnp.concatenate([_stack_heads(q[:, 0:128], 2, HEAD_DIM),
                            _stack_heads(q[:, 128:256], 2, HEAD_DIM)], axis=0)


def _gqa_merge(o):
    m = o.shape[0] // 4
    return jnp.concatenate([_unstack_heads(o[0:2 * m], 2, HEAD_DIM),
                            _unstack_heads(o[2 * m:4 * m], 2, HEAD_DIM)], axis=1)


def _gqa_kernel(q_ref, k_ref, v_ref, kc_ref, vc_ref, o_ref, s_ref, p_ref, l_ref, *, kb, rb):
    qs = _gqa_scores_lhs(q_ref[0])
    n = qs.shape[0]
    m = jnp.full((n, 1), NEG, F32)
    acc = jnp.zeros((n, 128), F32)
    l_ref[...] = jnp.zeros_like(l_ref)
    blocks = [(k_ref, v_ref, j * kb, kb) for j in range(k_ref.shape[1] // kb)]
    blocks.append((kc_ref, vc_ref, 0, kc_ref.shape[1]))
    for kr, vr, k0, size in blocks:
        s = _dot_nt(qs, kr[0, k0:k0 + size, :])
        s_ref[:, 0:size] = s
        m_new = jnp.maximum(m, s.max(axis=-1, keepdims=True))
        alpha = jnp.exp(m - m_new)
        for r0 in range(0, n, rb):
            p = jnp.exp(s_ref[r0:r0 + rb, 0:size] - m_new[r0:r0 + rb])
            part = p[:, 0:128]
            for j in range(1, size // 128):
                part = part + p[:, j * 128:(j + 1) * 128]
            l_ref[r0:r0 + rb, :] = l_ref[r0:r0 + rb, :] * alpha[r0:r0 + rb] + part
            p_ref[r0:r0 + rb, 0:size] = p.astype(BF16)
        acc = acc * alpha + _dot(p_ref[:, 0:size], vr[0, k0:k0 + size, :])
        m = m_new
    l = l_ref[...].sum(axis=-1, keepdims=True)
    o_ref[0] = _gqa_merge(acc * (1.0 / l)).astype(BF16)


def _gqa(q, k, v, kc, vc, tq, kb=512):
    nb, s, _ = q.shape
    lc = kc.shape[1]
    return pl.pallas_call(
        functools.partial(_gqa_kernel, kb=kb, rb=16),
        out_shape=jax.ShapeDtypeStruct((nb, s, 256), BF16),
        grid=(nb, s // tq),
        in_specs=[
            pl.BlockSpec((1, tq, 256), lambda b, t: (b, t, 0)),
            pl.BlockSpec((1, s, 128), lambda b, t: (b, 0, 0)),
            pl.BlockSpec((1, s, 128), lambda b, t: (b, 0, 0)),
            pl.BlockSpec((1, lc, 128), lambda b, t: (b, 0, 0)),
            pl.BlockSpec((1, lc, 128), lambda b, t: (b, 0, 0)),
        ],
        out_specs=pl.BlockSpec((1, tq, 256), lambda b, t: (b, t, 0)),
        scratch_shapes=[pltpu.VMEM((4 * tq, kb), F32), pltpu.VMEM((4 * tq, kb), BF16),
                        pltpu.VMEM((4 * tq, 128), F32)],
        compiler_params=_cparams("parallel", "parallel"),
        name="gqa_attn",
    )(q, k, v, kc, vc)


def _ctx_attn_kernel(naq_ref, nak_ref, nav_ref, q_ref, k_ref, v_ref, bz_ref, dz_ref):
    qs = _stack_heads(naq_ref[0], 4, HEAD_DIM)
    o = _softmax_pv([_dot_nt(qs, nak_ref[0])], [nav_ref[0]])
    bz_ref[0] = _unstack_heads(o, 4, HEAD_DIM).astype(BF16)
    qg = _gqa_scores_lhs(q_ref[0])
    og = _softmax_pv([_dot_nt(qg, k_ref[0])], [v_ref[0]])
    dz_ref[0] = _gqa_merge(og).astype(BF16)


def _ctx_attn(naq, nak, nav, q, k, v):
    nb, lc, _ = naq.shape

    def spec(w):
        return pl.BlockSpec((1, lc, w), lambda b: (b, 0, 0))

    out = jax.ShapeDtypeStruct((nb, lc, 256), BF16)
    return pl.pallas_call(
        _ctx_attn_kernel,
        out_shape=[out, out],
        grid=(nb,),
        in_specs=[spec(256), spec(256), spec(256), spec(256), spec(128), spec(128)],
        out_specs=[spec(256), spec(256)],
        compiler_params=_cparams("parallel"),
        name="context_attn",
    )(naq, nak, nav, q, k, v)


def _gla_kernel(qf_ref, vf_ref, qb_ref, vb_ref, z_ref, mask_ref, s0_ref, of_ref, ob_ref, sfin_ref,
                st_ref, *, chunks):
    p = pl.program_id(1)
    c = GLA_CHUNK

    @pl.when(p == 0)
    def _():
        st_ref[...] = s0_ref[0]

    row = lax.broadcasted_iota(jnp.int32, (256, 128), 0)
    lane = lax.broadcasted_iota(jnp.int32, (256, 128), 1)
    bd_mask = (row // HEAD_DIM) == (lane // GLA_DK)
    row4 = lax.broadcasted_iota(jnp.int32, (4 * c, 128), 0)
    lane4 = lax.broadcasted_iota(jnp.int32, (4 * c, 128), 1)
    head_mask = (row4 // c) == (lane4 // GLA_DK)
    n_lev = len(GLA_LEVELS)
    zq = jnp.zeros((4 * c, 128), BF16)
    zk = jnp.zeros((c, 128), BF16)

    units = []
    for ci in range(chunks):
        units.append((0, qf_ref, vf_ref, of_ref, ci * c, 256))
        units.append((1, qb_ref, vb_ref, ob_ref, (chunks - 1 - ci) * c, 384))

    loaded = []
    for d, in_ref, v_ref, o_ref, r0, a0 in units:
        q = in_ref[0, r0:r0 + c, 0:128]
        k = in_ref[0, r0:r0 + c, 128:256]
        a = in_ref[0, r0:r0 + c, a0:a0 + 128]
        v = v_ref[0, r0:r0 + c, :]
        a_hi = a.astype(BF16)
        a_lo = (a - a_hi.astype(F32)).astype(BF16)
        e2 = _dot(z_ref[d], jnp.concatenate([a_hi, a_lo], axis=1))
        x = jnp.exp(e2[:, 0:128] + e2[:, 128:256])
        loaded.append((q, k, v, x))

    staged = []
    for (d, _, _, _, _, _), (q, k, v, x) in zip(units, loaded):

        def level_operands(lev):
            if lev < n_lev:
                xl = x[(2 + lev) * c:(3 + lev) * c]
                ql, kl = q * xl, k * xl
            else:
                ql, kl = q, k
            qs = jnp.where(head_mask, jnp.concatenate([ql, ql, ql, ql], axis=0), 0.0)
            return qs.astype(BF16), kl.astype(BF16)

        scores = jnp.zeros((4 * c, 2 * c), F32)
        for pair in range((n_lev + 2) // 2):
            qa, ka = level_operands(2 * pair)
            qb, kb = level_operands(2 * pair + 1) if 2 * pair + 1 <= n_lev else (zq, zk)
            lhs = jnp.concatenate([qa, qb], axis=1)
            rhs = jnp.concatenate([jnp.concatenate([ka, zk], axis=1),
                                   jnp.concatenate([zk, kb], axis=1)], axis=0)
            scores = scores + _dot_nt(lhs, rhs) * mask_ref[d, pair]
        o_all = _dot(scores.astype(BF16), jnp.concatenate([v, v], axis=0))
        o_intra = _unstack_heads(o_all, 4, HEAD_DIM)
        qe = (q * x[0:c]).astype(BF16)
        ke = (k * x[c:2 * c]).astype(BF16)
        upd = jnp.where(bd_mask, _dot_tn(v, ke), 0.0)
        decay = x[8 * c:8 * c + 1, :]
        staged.append((o_intra, qe, upd, decay))

    st = [st_ref[0], st_ref[1]]
    for (d, _, _, o_ref, r0, _), (o_intra, qe, upd, decay) in zip(units, staged):
        o_ref[0, r0:r0 + c, :] = (o_intra + _dot_nt(qe, st[d].astype(BF16))).astype(BF16)
        st[d] = st[d] * decay + upd
    st_ref[0] = st[0]
    st_ref[1] = st[1]

    @pl.when(p == pl.num_programs(1) - 1)
    def _():
        sfin_ref[0] = st_ref[...]


def _gla(qka, v, s0, z_tab, masks):
    nb, r, _ = qka.shape
    chunks = 4
    blk = chunks * GLA_CHUNK
    nblk = r // blk
    out = jax.ShapeDtypeStruct((nb, r, 256), BF16)
    return pl.pallas_call(
        functools.partial(_gla_kernel, chunks=chunks),
        out_shape=[out, out, jax.ShapeDtypeStruct((nb, 2, 256, 128), F32)],
        grid=(nb, nblk),
        in_specs=[
            pl.BlockSpec((1, blk, 512), lambda b, p: (b, p, 0)),
            pl.BlockSpec((1, blk, 256), lambda b, p: (b, p, 0)),
            pl.BlockSpec((1, blk, 512), lambda b, p: (b, nblk - 1 - p, 0)),
            pl.BlockSpec((1, blk, 256), lambda b, p: (b, nblk - 1 - p, 0)),
            _const_spec(z_tab.shape),
            _const_spec(masks.shape),
            pl.BlockSpec((1, 2, 256, 128), lambda b, p: (b, 0, 0, 0)),
        ],
        out_specs=[
            pl.BlockSpec((1, blk, 256), lambda b, p: (b, p, 0)),
            pl.BlockSpec((1, blk, 256), lambda b, p: (b, nblk - 1 - p, 0)),
            pl.BlockSpec((1, 2, 256, 128), lambda b, p: (b, 0, 0, 0)),
        ],
        scratch_shapes=[pltpu.VMEM((2, 256, 128), F32)],
        compiler_params=_cparams("parallel", "arbitrary"),
        name="gla_scan",
    )(qka, v, qka, v, z_tab, masks, s0)


def _outproj_kernel(x_ref, mod_ref, a_ref, b_ref, of_ref, ob_ref, r_ref, d_ref, w_ref, g256_ref,
                    gn_ref, o_ref):
    o = of_ref[0].astype(F32) + ob_ref[0].astype(F32)
    cx = (_head_rms(o, g256_ref[...], gn_ref[...]) * r_ref[0].astype(F32)).astype(BF16)
    acc = _dot(a_ref[...], w_ref[0:256, :])
    acc = acc + _dot(b_ref[0], w_ref[256:512, :])
    acc = acc + _dot(cx, w_ref[512:768, :])
    acc = acc + _dot(d_ref[0], w_ref[768:1024, :])
    o_ref[0] = x_ref[0] + mod_ref[0, 5:6, :] * acc


def _outproj(x, mod, a, a_index, b, o_f, o_b, r, dd, w_out, g256, gn, tm):
    nb, rows, d = x.shape

    def act_spec(width):
        return pl.BlockSpec((1, tm, width), lambda bb, t: (bb, t, 0))

    return pl.pallas_call(
        _outproj_kernel,
        out_shape=jax.ShapeDtypeStruct(x.shape, F32),
        grid=(nb, rows // tm),
        in_specs=[
            act_spec(d),
            pl.BlockSpec((1, N_MOD, d), lambda bb, t: (bb, 0, 0)),
            pl.BlockSpec((tm, 256), a_index),
            act_spec(256), act_spec(256), act_spec(256), act_spec(256), act_spec(256),
            _const_spec(w_out.shape), _const_spec((256, 256)), _const_spec((1, 256)),
        ],
        out_specs=act_spec(d),
        compiler_params=_cparams("parallel", "parallel"),
        name="out_proj",
    )(x, mod, a, b, o_f, o_b, r, dd, w_out, g256, gn)


def _ffn_kernel(x_ref, mod_ref, w1_ref, w3_ref, w2_ref, o_ref, *, row0, fc):
    x = x_ref[0]
    hb = _norm_mod(x, mod_ref[0, row0:row0 + 1, :], mod_ref[0, row0 + 1:row0 + 2, :]).astype(BF16)
    o_ref[0] = x + (0.5 * mod_ref[0, row0 + 2:row0 + 3, :]) * _swiglu_acc(hb, w1_ref, w3_ref, w2_ref, fc)


def _ffn(x, mod, ffn_w, row0, tm):
    nb, r, d = x.shape
    w1, w3, w2 = ffn_w
    return pl.pallas_call(
        functools.partial(_ffn_kernel, row0=row0, fc=256),
        out_shape=jax.ShapeDtypeStruct(x.shape, F32),
        grid=(nb, r // tm),
        in_specs=[
            pl.BlockSpec((1, tm, d), lambda b, t: (b, t, 0)),
            pl.BlockSpec((1, N_MOD, d), lambda b, t: (b, 0, 0)),
            _const_spec(w1.shape), _const_spec(w3.shape), _const_spec(w2.shape),
        ],
        out_specs=pl.BlockSpec((1, tm, d), lambda b, t: (b, t, 0)),
        compiler_params=_cparams("parallel", "parallel"),
        name="ffn_halfstep",
    )(x, mod, w1, w3, w2)


def _dft_tables(l):
    r = 1 << (int(math.log2(l)) // 2)
    j = jnp.arange(l // 2, dtype=jnp.int32)
    k = jnp.concatenate([2 * j, 2 * j + 1])[None, :]

    def trig(n):
        ang = ((n[:, None] * k) % l).astype(F32) * (2.0 * math.pi / l)
        return jnp.cos(ang), jnp.sin(ang)

    ch, sh = trig(jnp.arange(l // 2 // r, dtype=jnp.int32) * r)
    cl, sl = trig(jnp.arange(r, dtype=jnp.int32))
    scale = l ** -0.5
    cos = (ch[:, None, :] * cl[None, :, :] - sh[:, None, :] * sl[None, :, :]).reshape(l // 2, l)
    sin = (sh[:, None, :] * cl[None, :, :] + ch[:, None, :] * sl[None, :, :]).reshape(l // 2, l)
    return (cos * scale).astype(BF16), (sin * scale).astype(BF16)


def _channel_dft_tables():
    n = jnp.arange(256, dtype=jnp.int32)
    same = (n[:, None] // HEAD_DIM) == (n[None, :] // HEAD_DIM)
    ang = (((n[:, None] % HEAD_DIM) * (n[None, :] % HEAD_DIM)) % HEAD_DIM).astype(F32) * (2.0 * math.pi / HEAD_DIM)
    scale = HEAD_DIM ** -0.5
    cc = jnp.where(same, jnp.cos(ang) * scale, 0.0).astype(BF16)
    sc = jnp.where(same, jnp.sin(ang) * scale, 0.0).astype(BF16)
    return cc, sc


def _group_matrix(width):
    n = jnp.arange(width, dtype=jnp.int32)
    return ((n[:, None] // HEAD_DIM) == (n[None, :] // HEAD_DIM)).astype(BF16)


def _rope_tables(n_tokens):
    t = jnp.arange(n_tokens, dtype=jnp.int32)
    row = (t // GRID_W).astype(F32)
    col = (t % GRID_W).astype(F32)
    n_freq = HEAD_DIM // 4
    inv_freq = ROPE_THETA ** (-jnp.arange(n_freq, dtype=F32) / n_freq)
    ang = jnp.concatenate([row[:, None] * inv_freq, col[:, None] * inv_freq], axis=-1)
    cos, sin = jnp.cos(ang), jnp.sin(ang)
    cos_t = jnp.concatenate([cos, cos, cos, cos], axis=-1)
    sin_t = jnp.concatenate([-sin, sin, -sin, sin], axis=-1)
    return cos_t, sin_t


def _gla_tables():
    c = GLA_CHUNK
    i = jnp.arange(c, dtype=jnp.int32)[:, None]
    t = jnp.arange(c, dtype=jnp.int32)[None, :]
    ones_row = jnp.ones((16, c), jnp.bool_)

    def tables(forward):
        z = [t <= i, t > i] if forward else [t >= i, t < i]
        m = []
        for s in GLA_LEVELS:
            pos = i % (2 * s)
            base = i - pos
            bound = base + s - 1
            right = pos >= s
            if forward:
                zq = right & (t > bound) & (t <= i)
                zk = (~right) & (t > i) & (t <= bound)
            else:
                zq = (~right) & (t >= i) & (t <= bound)
                zk = right & (t > bound) & (t < i)
            z.append(zq | zk)
            same_parent = (i // (2 * s)) == (t // (2 * s))
            t_right = (t % (2 * s)) >= s
            if forward:
                m.append(same_parent & right & (~t_right))
            else:
                m.append(same_parent & (~right) & t_right)
        m.append(i == t)
        z.append(ones_row)
        zt = jnp.concatenate(z, axis=0).astype(BF16)
        m.append(jnp.zeros_like(m[0]))
        mt = jnp.stack([jnp.tile(jnp.concatenate([m[2 * p], m[2 * p + 1]], axis=1), (4, 1))
                        for p in range(len(m) // 2)]).astype(F32)
        return zt, mt

    zf, mf = tables(True)
    zb, mb = tables(False)
    return jnp.stack([zf, zb]), jnp.stack([mf, mb])


def _na_bias_table(rpb):
    c = jnp.arange(GRID_W)[:, None, None]
    kc = jnp.arange(GRID_W)[None, :, None]
    m = jnp.arange(2 * NA_COLS - 1)[None, None, :]
    cs = jnp.clip(c - NA_COLS // 2, 0, GRID_W - NA_COLS)
    valid = (kc >= cs) & (kc < cs + NA_COLS)
    pick = (valid & (kc - c + NA_COLS - 1 == m)).astype(F32)
    by_row = jnp.einsum('hrm,ckm->rhck', rpb.astype(F32), pick, precision=lax.Precision.HIGHEST)
    by_row = jnp.where(valid[None, None, :, :, 0], by_row, NEG)
    blocks = [jnp.concatenate([by_row[j - off + NA_ROWS - 1] for j in range(NA_ROWS)], axis=-1)
              for off in range(NA_ROWS)]
    return jnp.stack(blocks).reshape(NA_ROWS, 4 * GRID_W, NA_ROWS * GRID_W)


_GQA_PERM = (0, 2, 1, 3)


def _layer_weights(l, w_in, na_q_norm, na_k_norm, gla_w_gate_f, gla_b_gate_f, gla_w_gate_b, gla_b_gate_b,
                   gla_norm, gqa_q_norm, gqa_k_norm, w_out):
    wi = w_in[l]
    d = wi.shape[0]
    gq = wi[:, 1824:2080].reshape(d, 4, HEAD_DIM)[:, _GQA_PERM, :].reshape(d, 256)
    w_in_r = jnp.concatenate([wi[:, 0:1792], gq, wi[:, 2080:2336], wi[:, 1792:1824],
                              jnp.zeros((d, 96), F32)], axis=1).astype(BF16)
    wg = jnp.zeros((128, 256), F32)
    wg = wg.at[0:16, 0:128].set(gla_w_gate_f[l]).at[16:32, 128:256].set(gla_w_gate_b[l]).astype(BF16)
    bg = jnp.concatenate([gla_b_gate_f[l], gla_b_gate_b[l]])[None, :]
    wo = w_out[l]
    wo_d = wo[768:1024].reshape(4, HEAD_DIM, -1)[_GQA_PERM, :, :].reshape(256, -1)
    w_out_r = jnp.concatenate([wo[0:768], wo_d], axis=0).astype(BF16)

    def tile4(g):
        return jnp.tile(g, 4)[None, :]

    return {
        "w_in": w_in_r, "wg": wg, "bg": bg, "w_out": w_out_r,
        "naq_g": tile4(na_q_norm[l]), "nak_g": tile4(na_k_norm[l]),
        "gq_g": tile4(gqa_q_norm[l]), "gk_g": tile4(gqa_k_norm[l]),
        "gn": tile4(gla_norm[l]),
    }


def kernel(x, c, ctx, c_ctx, w_mod, b_mod, ffn1_w1, ffn1_w3, ffn1_w2, w_in, na_q_norm, na_k_norm, na_rpb,
           gla_w_gate_f, gla_b_gate_f, gla_w_gate_b, gla_b_gate_b, gla_norm, gqa_q_norm, gqa_k_norm, w_out,
           ffn2_w1, ffn2_w3, ffn2_w2):
    nb, s, d = x.shape
    lc = ctx.shape[1]
    depth = w_mod.shape[0]
    tm_x = 512
    tm_z = 256

    cc = jnp.zeros((16, d), F32).at[0:nb].set(c).at[nb].set(c_ctx)
    mod_all = _modulation(cc, w_mod, b_mod)

    consts = {"g256": _group_matrix(256), "g128": _group_matrix(128)}
    consts["cc"], consts["sc"] = _channel_dft_tables()
    cos_t, sin_t = _rope_tables(s)
    dft_x = _dft_tables(s)
    dft_z = _dft_tables(lc)
    z_tab, gla_masks = _gla_tables()
    s_zero = jnp.zeros((nb, 2, 256, 128), F32)

    z = ctx.reshape(1, nb * lc, d)
    four_x = (lambda b, t: (0, t, b), (2, s // 2, nb * 256))
    four_z = (lambda b, t: (0, 0, t), (2, lc // 2, nb * 256))

    def mix_x(b, t):
        return (t, b)

    def mix_z(b, t):
        return (0, t)

    for l in range(depth):
        last = l == depth - 1
        mod = mod_all[l, 0:nb + 1].reshape(nb + 1, N_MOD, d)
        mod_x, mod_z = mod[0:nb], mod[nb:nb + 1]
        lw = _layer_weights(l, w_in, na_q_norm, na_k_norm, gla_w_gate_f, gla_b_gate_f, gla_w_gate_b,
                            gla_b_gate_b, gla_norm, gqa_q_norm, gqa_k_norm, w_out)
        f1 = (ffn1_w1[l].astype(BF16), ffn1_w3[l].astype(BF16), ffn1_w2[l].astype(BF16))
        f2 = (ffn2_w1[l].astype(BF16), ffn2_w3[l].astype(BF16), ffn2_w2[l].astype(BF16))
        tb = _na_bias_table(na_rpb[l])

        px = _ffn_inproj(x, mod_x, f1, lw, consts, cos_t, sin_t, True, tm_x, four_x)
        pz = _ffn_inproj(z, mod_z, f1, lw, consts, cos_t, sin_t, False, tm_z, four_z)
        (x, ucx, usx, naq_x, nak_x, nav_x, qka_x, gv_x, gr_x, q_x, k_x, v_x) = px
        z = pz[0]
        (ucz, usz, naq_z, nak_z, nav_z, qka_z, gv_z, gr_z, q_z, k_z, v_z) = [
            t if t.shape[0] == 2 else t.reshape(nb, lc, t.shape[-1]) for t in pz[1:]]

        a_x = _dft(dft_x[0], dft_x[1], ucx, usx, tm=256)
        b_x = _na(naq_x, nak_x, nav_x, nak_z, nav_z, tb)
        ofz, obz, s_ctx = _gla(qka_z, gv_z, s_zero, z_tab, gla_masks)
        ofx, obx, _ = _gla(qka_x, gv_x, s_ctx, z_tab, gla_masks)
        d_x = _gqa(q_x, k_x, v_x, k_z, v_z, tq=256, kb=2048)

        x = _outproj(x, mod_x, a_x, mix_x, b_x, ofx, obx, gr_x, d_x, lw["w_out"], consts["g256"],
                     lw["gn"], tm_x)
        x = _ffn(x, mod_x, f2, row0=6, tm=tm_x)

        if not last:
            a_z = _dft(dft_z[0], dft_z[1], ucz, usz, tm=lc // 2)
            b_z, d_z = _ctx_attn(naq_z, nak_z, nav_z, q_z, k_z, v_z)
            z = _outproj(z, mod_z, a_z, mix_z, b_z.reshape(1, nb * lc, 256),
                         ofz.reshape(1, nb * lc, 256), obz.reshape(1, nb * lc, 256),
                         gr_z.reshape(1, nb * lc, 256), d_z.reshape(1, nb * lc, 256),
                         lw["w_out"], consts["g256"], lw["gn"], tm_z)
            z = _ffn(z, mod_z, f2, row0=6, tm=tm_z)
    return x
```

```python
import functools
import math

import jax
import jax.numpy as jnp
from jax import lax
from jax.experimental import pallas as pl
from jax.experimental.pallas import tpu as pltpu

F32 = jnp.float32
BF16 = jnp.bfloat16

HEAD_DIM = 64
GRID_W = 64
NA_ROWS = 8
NA_COLS = 16
GLA_DK = 32
GLA_TAU = 16.0
GLA_CHUNK = 64
GLA_LEVELS = (32, 16, 8, 4, 2, 1)
ROPE_THETA = 10000.0
EPS = 1e-6
N_MOD = 9
NEG = -1e30

VMEM_LIMIT = 56 * 1024 * 1024


def _cparams(*sem):
    return pltpu.CompilerParams(dimension_semantics=sem, vmem_limit_bytes=VMEM_LIMIT)


def _dot(a, b):
    return jnp.dot(a, b, preferred_element_type=F32)


def _dot_nt(a, b):
    return lax.dot_general(a, b, (((1,), (1,)), ((), ())), preferred_element_type=F32)


def _dot_tn(a, b):
    return lax.dot_general(a, b, (((0,), (0,)), ((), ())), preferred_element_type=F32)


def _sigmoid(x):
    return 1.0 / (1.0 + jnp.exp(-x))


def _silu(x):
    return x * _sigmoid(x)


def _log_sigmoid(x):
    return jnp.minimum(x, 0.0) - jnp.log(1.0 + jnp.exp(-jnp.abs(x)))


def _const_spec(shape):
    nd = len(shape)
    return pl.BlockSpec(shape, lambda *_: (0,) * nd, pipeline_mode=pl.Buffered(1))


def _mod_kernel(c_ref, w_ref, b_ref, o_ref):
    h = _silu(c_ref[...]).astype(BF16)
    o_ref[0] = _dot(h, w_ref[0].astype(BF16)) + b_ref[0]


def _modulation(cc, w_mod, b_mod):
    depth, d, n = w_mod.shape
    rows = cc.shape[0]
    tn = 1024
    return pl.pallas_call(
        _mod_kernel,
        out_shape=jax.ShapeDtypeStruct((depth, rows, n), F32),
        grid=(depth, n // tn),
        in_specs=[
            pl.BlockSpec((rows, d), lambda l, j: (0, 0)),
            pl.BlockSpec((1, d, tn), lambda l, j: (l, 0, j)),
            pl.BlockSpec((1, 1, tn), lambda l, j: (l, 0, j)),
        ],
        out_specs=pl.BlockSpec((1, rows, tn), lambda l, j: (l, 0, j)),
        compiler_params=_cparams("parallel", "parallel"),
        name="adaln_mod",
    )(cc, w_mod, b_mod.reshape(depth, 1, n))


def _norm_mod(x, shift, scale):
    h = x * lax.rsqrt(jnp.mean(x * x, axis=-1, keepdims=True) + EPS)
    return h * (1.0 + scale) + shift


def _swiglu_acc(hb, w1_ref, w3_ref, w2_ref, fc):
    d_ff = w1_ref.shape[1]
    acc = jnp.zeros((hb.shape[0], w2_ref.shape[1]), F32)
    for f0 in range(0, d_ff, fc):
        a = _dot(hb, w1_ref[:, f0:f0 + fc].astype(BF16))
        b = _dot(hb, w3_ref[:, f0:f0 + fc].astype(BF16))
        g = (_silu(a) * b).astype(BF16)
        acc = acc + _dot(g, w2_ref[f0:f0 + fc, :].astype(BF16))
    return acc


def _group_sum(xsq, gmat):
    hi = xsq.astype(BF16)
    lo = (xsq - hi.astype(F32)).astype(BF16)
    return _dot(hi, gmat) + _dot(lo, gmat)


def _head_rms(x, gmat, gain):
    ms = _group_sum(x * x, gmat) * (1.0 / HEAD_DIM)
    return x * lax.rsqrt(ms + EPS) * gain


def _rope128(x, cos, sin_signed):
    lane = lax.broadcasted_iota(jnp.int32, x.shape, 1)
    first_half = (lane % HEAD_DIM) < (HEAD_DIM // 2)
    partner = jnp.where(first_half, pltpu.roll(x, 96, 1), pltpu.roll(x, 32, 1))
    return x * cos + partner * sin_signed


def _inproj_kernel(x_ref, mod_ref, w1_ref, w3_ref, w2_ref, w_ref, g256_ref, g128_ref, cc_ref, sc_ref,
                   naq_g_ref, nak_g_ref, gq_g_ref, gk_g_ref, wg_ref, bg_ref, cos_ref, sin_ref,
                   x1_ref, uc_ref, us_ref, naq_ref, nak_ref, nav_ref, qka_ref, gv_ref, gr_ref,
                   q_ref, k_ref, v_ref, four_ref, *, rope, fc):
    x0 = x_ref[0]
    hb0 = _norm_mod(x0, mod_ref[0, 0:1, :], mod_ref[0, 1:2, :]).astype(BF16)
    x = x0 + (0.5 * mod_ref[0, 2:3, :]) * _swiglu_acc(hb0, w1_ref, w3_ref, w2_ref, fc)
    x1_ref[0] = x
    shift = mod_ref[0, 3:4, :]
    scale = mod_ref[0, 4:5, :]
    hb = _norm_mod(x, shift, scale).astype(BF16)
    h = _dot(hb, w_ref[...])
    g256 = g256_ref[...]
    g128 = g128_ref[...]
    ub = h[:, 0:256].astype(BF16)
    half = four_ref.shape[1] // 2
    for w, (out_ref, tab_ref) in enumerate(((uc_ref, cc_ref), (us_ref, sc_ref))):
        u = _dot(ub, tab_ref[...])
        for j in range(2):
            s = 2 * w + j
            four_ref[s] = u[:, j * 128:(j + 1) * 128]
            out_ref[0, :, j * 128:(j + 1) * 128] = four_ref[s, pl.ds(0, half, stride=2), :].astype(BF16)
            out_ref[1, :, j * 128:(j + 1) * 128] = four_ref[s, pl.ds(1, half, stride=2), :].astype(BF16)
    naq_ref[0] = (_head_rms(h[:, 256:512], g256, naq_g_ref[...]) * HEAD_DIM ** -0.5).astype(BF16)
    nak_ref[0] = _head_rms(h[:, 512:768], g256, nak_g_ref[...]).astype(BF16)
    nav_ref[0] = h[:, 768:1024].astype(BF16)
    gates = _dot(h[:, 2304:2432].astype(BF16), wg_ref[...]) + bg_ref[...]
    log_a = _log_sigmoid(gates) / GLA_TAU
    qka_ref[0, :, 0:128] = h[:, 1024:1152] * GLA_DK ** -0.5
    qka_ref[0, :, 128:256] = h[:, 1152:1280]
    qka_ref[0, :, 256:512] = log_a
    gv_ref[0] = h[:, 1280:1536].astype(BF16)
    gr_ref[0] = _silu(h[:, 1536:1792]).astype(BF16)
    q = _head_rms(h[:, 1792:2048], g256, gq_g_ref[...])
    k = _head_rms(h[:, 2048:2176], g128, gk_g_ref[:, 0:128])
    if rope:
        cos = cos_ref[...]
        sin = sin_ref[...]
        q = jnp.concatenate([_rope128(q[:, 0:128], cos, sin), _rope128(q[:, 128:256], cos, sin)], axis=1)
        k = _rope128(k, cos, sin)
    q_ref[0] = (q * HEAD_DIM ** -0.5).astype(BF16)
    k_ref[0] = k.astype(BF16)
    v_ref[0] = h[:, 2176:2304].astype(BF16)


def _ffn_inproj(x, mod, ffn_w, lw, consts, cos_t, sin_t, rope, tm, four_index):
    nb, r, d = x.shape
    n_four = four_index[1]
    w1, w3, w2 = ffn_w

    def act(width, dtype):
        return jax.ShapeDtypeStruct((nb, r, width), dtype)

    def act_spec(width):
        return pl.BlockSpec((1, tm, width), lambda b, t: (b, t, 0))

    four_shape = jax.ShapeDtypeStruct(n_four, BF16)
    four_spec = pl.BlockSpec((2, tm // 2, 256), four_index[0])
    vec256 = _const_spec((1, 256))
    out_shapes = [act(d, F32), four_shape, four_shape, act(256, BF16), act(256, BF16), act(256, BF16),
                  act(512, F32), act(256, BF16), act(256, BF16),
                  act(256, BF16), act(128, BF16), act(128, BF16)]
    out_specs = [act_spec(d), four_spec, four_spec, act_spec(256), act_spec(256), act_spec(256),
                 act_spec(512), act_spec(256), act_spec(256),
                 act_spec(256), act_spec(128), act_spec(128)]
    return pl.pallas_call(
        functools.partial(_inproj_kernel, rope=rope, fc=256),
        out_shape=out_shapes,
        grid=(nb, r // tm),
        in_specs=[
            pl.BlockSpec((1, tm, d), lambda b, t: (b, t, 0)),
            pl.BlockSpec((1, N_MOD, d), lambda b, t: (b, 0, 0)),
            _const_spec(w1.shape), _const_spec(w3.shape), _const_spec(w2.shape),
            _const_spec(lw["w_in"].shape),
            _const_spec((256, 256)), _const_spec((128, 128)),
            _const_spec((256, 256)), _const_spec((256, 256)),
            vec256, vec256, vec256, vec256,
            _const_spec((128, 256)), vec256,
            pl.BlockSpec((tm, 128), lambda b, t: (t, 0)),
            pl.BlockSpec((tm, 128), lambda b, t: (t, 0)),
        ],
        out_specs=out_specs,
        scratch_shapes=[pltpu.VMEM((4, tm, 128), F32)],
        compiler_params=_cparams("parallel", "parallel"),
        name="ffn1_in_proj",
    )(x, mod, w1, w3, w2, lw["w_in"], consts["g256"], consts["g128"], consts["cc"], consts["sc"],
      lw["naq_g"], lw["nak_g"], lw["gq_g"], lw["gk_g"], lw["wg"], lw["bg"], cos_t, sin_t)


def _dft_kernel(c_ref, s_ref, uc_ref, us_ref, lo_ref, hi_ref):
    h = uc_ref.shape[1]
    a_e = _dot(c_ref[:, 0:h], uc_ref[0]) - _dot(s_ref[:, 0:h], us_ref[0])
    a_o = _dot(c_ref[:, h:2 * h], uc_ref[1]) - _dot(s_ref[:, h:2 * h], us_ref[1])
    lo_ref[...] = (a_e + a_o).astype(lo_ref.dtype)
    hi_ref[...] = (a_e - a_o).astype(hi_ref.dtype)


def _dft(c_tab, s_tab, uc, us, tm):
    _, h, n = uc.shape
    half = jax.ShapeDtypeStruct((h, n), BF16)
    lo, hi = pl.pallas_call(
        _dft_kernel,
        out_shape=[half, half],
        grid=(h // tm,),
        in_specs=[
            pl.BlockSpec((tm, 2 * h), lambda i: (i, 0)),
            pl.BlockSpec((tm, 2 * h), lambda i: (i, 0)),
            _const_spec(uc.shape),
            _const_spec(us.shape),
        ],
        out_specs=[pl.BlockSpec((tm, n), lambda i: (i, 0)), pl.BlockSpec((tm, n), lambda i: (i, 0))],
        compiler_params=_cparams("parallel"),
        name="position_dft",
    )(c_tab, s_tab, uc, us)
    return jnp.concatenate([lo, hi], axis=0)


def _stack_heads(q, n_heads, width):
    lane = lax.broadcasted_iota(jnp.int32, q.shape, 1)
    zero = jnp.zeros_like(q)
    return jnp.concatenate([jnp.where(lane // width == h, q, zero) for h in range(n_heads)], axis=0)


def _unstack_heads(o, n_heads, width):
    m = o.shape[0] // n_heads
    lane = lax.broadcasted_iota(jnp.int32, (m, o.shape[1]), 1)
    out = jnp.zeros((m, o.shape[1]), F32)
    for h in range(n_heads):
        out = out + jnp.where(lane // width == h, o[h * m:(h + 1) * m, :], 0.0)
    return out


def _softmax_pv(score_blocks, value_blocks):
    m = score_blocks[0].max(axis=-1, keepdims=True)
    for s in score_blocks[1:]:
        m = jnp.maximum(m, s.max(axis=-1, keepdims=True))
    l = None
    o = None
    for s, v in zip(score_blocks, value_blocks):
        p = jnp.exp(s - m)
        ps = p.sum(axis=-1, keepdims=True)
        pv = _dot(p.astype(BF16), v)
        l = ps if l is None else l + ps
        o = pv if o is None else o + pv
    return o * (1.0 / l)


def _na_kernel(q_ref, k_ref, v_ref, kc_ref, vc_ref, tb_ref, o_ref, *, rows_per_step, n_rows):
    rb = pl.program_id(1)
    kc = kc_ref[0]
    vc = vc_ref[0]
    for i in range(rows_per_step):
        r = rb * rows_per_step + i
        rs = jnp.clip(r - NA_ROWS // 2, 0, n_rows - NA_ROWS)
        off = r - rs
        q = q_ref[0, i * GRID_W:(i + 1) * GRID_W, :]
        qs = _stack_heads(q, 4, HEAD_DIM)
        start = pl.multiple_of(rs * GRID_W, GRID_W)
        kw = k_ref[0, pl.ds(start, NA_ROWS * GRID_W), :]
        vw = v_ref[0, pl.ds(start, NA_ROWS * GRID_W), :]
        s_w = _dot_nt(qs, kw) + tb_ref[off]
        s_c = _dot_nt(qs, kc)
        o = _softmax_pv([s_w, s_c], [vw, vc])
        o_ref[0, i * GRID_W:(i + 1) * GRID_W, :] = _unstack_heads(o, 4, HEAD_DIM).astype(BF16)


def _na(q, k, v, kc, vc, tb):
    nb, s, w = q.shape
    n_rows = s // GRID_W
    rps = 16
    lc = kc.shape[1]
    return pl.pallas_call(
        functools.partial(_na_kernel, rows_per_step=rps, n_rows=n_rows),
        out_shape=jax.ShapeDtypeStruct((nb, s, w), BF16),
        grid=(nb, n_rows // rps),
        in_specs=[
            pl.BlockSpec((1, rps * GRID_W, w), lambda b, t: (b, t, 0)),
            pl.BlockSpec((1, s, w), lambda b, t: (b, 0, 0)),
            pl.BlockSpec((1, s, w), lambda b, t: (b, 0, 0)),
            pl.BlockSpec((1, lc, w), lambda b, t: (b, 0, 0)),
            pl.BlockSpec((1, lc, w), lambda b, t: (b, 0, 0)),
            _const_spec(tb.shape),
        ],
        out_specs=pl.BlockSpec((1, rps * GRID_W, w), lambda b, t: (b, t, 0)),
        compiler_params=_cparams("parallel", "parallel"),
        name="neighborhood_attn",
    )(q, k, v, kc, vc, tb)


def _gqa_scores_lhs(q):
    return jnp.concatenate([_stack_heads(q[:, 0:128], 2, HEAD_DIM),
                            _stack_heads(q[:, 128:256], 2, HEAD_DIM)], axis=0)


def _gqa_merge(o):
    m = o.shape[0] // 4
    return jnp.concatenate([_unstack_heads(o[0:2 * m], 2, HEAD_DIM),
                            _unstack_heads(o[2 * m:4 * m], 2, HEAD_DIM)], axis=1)


def _gqa_kernel(q_ref, k_ref, v_ref, kc_ref, vc_ref, o_ref, s_ref, p_ref, l_ref, *, kb, rb):
    qs = _gqa_scores_lhs(q_ref[0])
    n = qs.shape[0]
    m = jnp.full((n, 1), NEG, F32)
    acc = jnp.zeros((n, 128), F32)
    l_ref[...] = jnp.zeros_like(l_ref)
    blocks = [(k_ref, v_ref, j * kb, kb) for j in range(k_ref.shape[1] // kb)]
    blocks.append((kc_ref, vc_ref, 0, kc_ref.shape[1]))
    for kr, vr, k0, size in blocks:
        s = _dot_nt(qs, kr[0, k0:k0 + size, :])
        s_ref[:, 0:size] = s
        m_new = jnp.maximum(m, s.max(axis=-1, keepdims=True))
        alpha = jnp.exp(m - m_new)
        for r0 in range(0, n, rb):
            p = jnp.exp(s_ref[r0:r0 + rb, 0:size] - m_new[r0:r0 + rb])
            part = p[:, 0:128]
            for j in range(1, size // 128):
                part = part + p[:, j * 128:(j + 1) * 128]
            l_ref[r0:r0 + rb, :] = l_ref[r0:r0 + rb, :] * alpha[r0:r0 + rb] + part
            p_ref[r0:r0 + rb, 0:size] = p.astype(BF16)
        acc = acc * alpha + _dot(p_ref[:, 0:size], vr[0, k0:k0 + size, :])
        m = m_new
    l = l_ref[...].sum(axis=-1, keepdims=True)
    o_ref[0] = _gqa_merge(acc * (1.0 / l)).astype(BF16)


def _gqa(q, k, v, kc, vc, tq, kb=512):
    nb, s, _ = q.shape
    lc = kc.shape[1]
    return pl.pallas_call(
        functools.partial(_gqa_kernel, kb=kb, rb=16),
        out_shape=jax.ShapeDtypeStruct((nb, s, 256), BF16),
        grid=(nb, s // tq),
        in_specs=[
            pl.BlockSpec((1, tq, 256), lambda b, t: (b, t, 0)),
            pl.BlockSpec((1, s, 128), lambda b, t: (b, 0, 0)),
            pl.BlockSpec((1, s, 128), lambda b, t: (b, 0, 0)),
            pl.BlockSpec((1, lc, 128), lambda b, t: (b, 0, 0)),
            pl.BlockSpec((1, lc, 128), lambda b, t: (b, 0, 0)),
        ],
        out_specs=pl.BlockSpec((1, tq, 256), lambda b, t: (b, t, 0)),
        scratch_shapes=[pltpu.VMEM((4 * tq, kb), F32), pltpu.VMEM((4 * tq, kb), BF16),
                        pltpu.VMEM((4 * tq, 128), F32)],
        compiler_params=_cparams("parallel", "parallel"),
        name="gqa_attn",
    )(q, k, v, kc, vc)


def _ctx_attn_kernel(naq_ref, nak_ref, nav_ref, q_ref, k_ref, v_ref, bz_ref, dz_ref):
    qs = _stack_heads(naq_ref[0], 4, HEAD_DIM)
    o = _softmax_pv([_dot_nt(qs, nak_ref[0])], [nav_ref[0]])
    bz_ref[0] = _unstack_heads(o, 4, HEAD_DIM).astype(BF16)
    qg = _gqa_scores_lhs(q_ref[0])
    og = _softmax_pv([_dot_nt(qg, k_ref[0])], [v_ref[0]])
    dz_ref[0] = _gqa_merge(og).astype(BF16)


def _ctx_attn(naq, nak, nav, q, k, v):
    nb, lc, _ = naq.shape

    def spec(w):
        return pl.BlockSpec((1, lc, w), lambda b: (b, 0, 0))

    out = jax.ShapeDtypeStruct((nb, lc, 256), BF16)
    return pl.pallas_call(
        _ctx_attn_kernel,
        out_shape=[out, out],
        grid=(nb,),
        in_specs=[spec(256), spec(256), spec(256), spec(256), spec(128), spec(128)],
        out_specs=[spec(256), spec(256)],
        compiler_params=_cparams("parallel"),
        name="context_attn",
    )(naq, nak, nav, q, k, v)


def _gla_kernel(qf_ref, vf_ref, qb_ref, vb_ref, z_ref, mask_ref, s0_ref, of_ref, ob_ref, sfin_ref,
                st_ref, *, chunks):
    p = pl.program_id(1)
    c = GLA_CHUNK

    @pl.when(p == 0)
    def _():
        st_ref[...] = s0_ref[0]

    row = lax.broadcasted_iota(jnp.int32, (256, 128), 0)
    lane = lax.broadcasted_iota(jnp.int32, (256, 128), 1)
    bd_mask = (row // HEAD_DIM) == (lane // GLA_DK)
    row4 = lax.broadcasted_iota(jnp.int32, (4 * c, 128), 0)
    lane4 = lax.broadcasted_iota(jnp.int32, (4 * c, 128), 1)
    head_mask = (row4 // c) == (lane4 // GLA_DK)
    n_lev = len(GLA_LEVELS)
    zq = jnp.zeros((4 * c, 128), BF16)
    zk = jnp.zeros((c, 128), BF16)

    units = []
    for ci in range(chunks):
        units.append((0, qf_ref, vf_ref, of_ref, ci * c, 256))
        units.append((1, qb_ref, vb_ref, ob_ref, (chunks - 1 - ci) * c, 384))

    loaded = []
    for d, in_ref, v_ref, o_ref, r0, a0 in units:
        q = in_ref[0, r0:r0 + c, 0:128]
        k = in_ref[0, r0:r0 + c, 128:256]
        a = in_ref[0, r0:r0 + c, a0:a0 + 128]
        v = v_ref[0, r0:r0 + c, :]
        a_hi = a.astype(BF16)
        a_lo = (a - a_hi.astype(F32)).astype(BF16)
        e2 = _dot(z_ref[d], jnp.concatenate([a_hi, a_lo], axis=1))
        x = jnp.exp(e2[:, 0:128] + e2[:, 128:256])
        loaded.append((q, k, v, x))

    staged = []
    for (d, _, _, _, _, _), (q, k, v, x) in zip(units, loaded):

        def level_operands(lev):
            if lev < n_lev:
                xl = x[(2 + lev) * c:(3 + lev) * c]
                ql, kl = q * xl, k * xl
            else:
                ql, kl = q, k
            qs = jnp.where(head_mask, jnp.concatenate([ql, ql, ql, ql], axis=0), 0.0)
            return qs.astype(BF16), kl.astype(BF16)

        scores = jnp.zeros((4 * c, 2 * c), F32)
        for pair in range((n_lev + 2) // 2):
            qa, ka = level_operands(2 * pair)
            qb, kb = level_operands(2 * pair + 1) if 2 * pair + 1 <= n_lev else (zq, zk)
            lhs = jnp.concatenate([qa, qb], axis=1)
            rhs = jnp.concatenate([jnp.concatenate([ka, zk], axis=1),
                                   jnp.concatenate([zk, kb], axis=1)], axis=0)
            scores = scores + _dot_nt(lhs, rhs) * mask_ref[d, pair]
        o_all = _dot(scores.astype(BF16), jnp.concatenate([v, v], axis=0))
        o_intra = _unstack_heads(o_all, 4, HEAD_DIM)
        qe = (q * x[0:c]).astype(BF16)
        ke = (k * x[c:2 * c]).astype(BF16)
        upd = jnp.where(bd_mask, _dot_tn(v, ke), 0.0)
        decay = x[8 * c:8 * c + 1, :]
        staged.append((o_intra, qe, upd, decay))

    st = [st_ref[0], st_ref[1]]
    for (d, _, _, o_ref, r0, _), (o_intra, qe, upd, decay) in zip(units, staged):
        o_ref[0, r0:r0 + c, :] = (o_intra + _dot_nt(qe, st[d].astype(BF16))).astype(BF16)
        st[d] = st[d] * decay + upd
    st_ref[0] = st[0]
    st_ref[1] = st[1]

    @pl.when(p == pl.num_programs(1) - 1)
    def _():
        sfin_ref[0] = st_ref[...]


def _gla(qka, v, s0, z_tab, masks, chunks):
    nb, r, _ = qka.shape
    blk = chunks * GLA_CHUNK
    nblk = r // blk
    out = jax.ShapeDtypeStruct((nb, r, 256), BF16)
    return pl.pallas_call(
        functools.partial(_gla_kernel, chunks=chunks),
        out_shape=[out, out, jax.ShapeDtypeStruct((nb, 2, 256, 128), F32)],
        grid=(nb, nblk),
        in_specs=[
            pl.BlockSpec((1, blk, 512), lambda b, p: (b, p, 0)),
            pl.BlockSpec((1, blk, 256), lambda b, p: (b, p, 0)),
            pl.BlockSpec((1, blk, 512), lambda b, p: (b, nblk - 1 - p, 0)),
            pl.BlockSpec((1, blk, 256), lambda b, p: (b, nblk - 1 - p, 0)),
            _const_spec(z_tab.shape),
            _const_spec(masks.shape),
            pl.BlockSpec((1, 2, 256, 128), lambda b, p: (b, 0, 0, 0)),
        ],
        out_specs=[
            pl.BlockSpec((1, blk, 256), lambda b, p: (b, p, 0)),
            pl.BlockSpec((1, blk, 256), lambda b, p: (b, nblk - 1 - p, 0)),
            pl.BlockSpec((1, 2, 256, 128), lambda b, p: (b, 0, 0, 0)),
        ],
        scratch_shapes=[pltpu.VMEM((2, 256, 128), F32)],
        compiler_params=_cparams("parallel", "arbitrary"),
        name="gla_scan",
    )(qka, v, qka, v, z_tab, masks, s0)


def _outproj_kernel(x_ref, mod_ref, a_ref, b_ref, of_ref, ob_ref, r_ref, d_ref, w_ref, g256_ref,
                    gn_ref, o_ref):
    o = of_ref[0].astype(F32) + ob_ref[0].astype(F32)
    cx = (_head_rms(o, g256_ref[...], gn_ref[...]) * r_ref[0].astype(F32)).astype(BF16)
    acc = _dot(a_ref[...], w_ref[0:256, :])
    acc = acc + _dot(b_ref[0], w_ref[256:512, :])
    acc = acc + _dot(cx, w_ref[512:768, :])
    acc = acc + _dot(d_ref[0], w_ref[768:1024, :])
    o_ref[0] = x_ref[0] + mod_ref[0, 5:6, :] * acc


def _outproj(x, mod, a, a_index, b, o_f, o_b, r, dd, w_out, g256, gn, tm):
    nb, rows, d = x.shape

    def act_spec(width):
        return pl.BlockSpec((1, tm, width), lambda bb, t: (bb, t, 0))

    return pl.pallas_call(
        _outproj_kernel,
        out_shape=jax.ShapeDtypeStruct(x.shape, F32),
        grid=(nb, rows // tm),
        in_specs=[
            act_spec(d),
            pl.BlockSpec((1, N_MOD, d), lambda bb, t: (bb, 0, 0)),
            pl.BlockSpec((tm, 256), a_index),
            act_spec(256), act_spec(256), act_spec(256), act_spec(256), act_spec(256),
            _const_spec(w_out.shape), _const_spec((256, 256)), _const_spec((1, 256)),
        ],
        out_specs=act_spec(d),
        compiler_params=_cparams("parallel", "parallel"),
        name="out_proj",
    )(x, mod, a, b, o_f, o_b, r, dd, w_out, g256, gn)


def _ffn_kernel(x_ref, mod_ref, w1_ref, w3_ref, w2_ref, o_ref, *, row0, fc):
    x = x_ref[0]
    hb = _norm_mod(x, mod_ref[0, row0:row0 + 1, :], mod_ref[0, row0 + 1:row0 + 2, :]).astype(BF16)
    o_ref[0] = x + (0.5 * mod_ref[0, row0 + 2:row0 + 3, :]) * _swiglu_acc(hb, w1_ref, w3_ref, w2_ref, fc)


def _ffn(x, mod, ffn_w, layer, row0, tm):
    nb, r, d = x.shape
    w1, w3, w2 = ffn_w

    def layer_spec(w):
        return pl.BlockSpec((None,) + w.shape[1:], lambda b, t: (layer, 0, 0), pipeline_mode=pl.Buffered(1))

    return pl.pallas_call(
        functools.partial(_ffn_kernel, row0=row0, fc=256),
        out_shape=jax.ShapeDtypeStruct(x.shape, F32),
        grid=(nb, r // tm),
        in_specs=[
            pl.BlockSpec((1, tm, d), lambda b, t: (b, t, 0)),
            pl.BlockSpec((1, N_MOD, d), lambda b, t: (b, 0, 0)),
            layer_spec(w1), layer_spec(w3), layer_spec(w2),
        ],
        out_specs=pl.BlockSpec((1, tm, d), lambda b, t: (b, t, 0)),
        compiler_params=_cparams("parallel", "parallel"),
        name="ffn_halfstep",
    )(x, mod, w1, w3, w2)


def _dft_tables(l):
    r = 1 << (int(math.log2(l)) // 2)
    j = jnp.arange(l // 2, dtype=jnp.int32)
    k = jnp.concatenate([2 * j, 2 * j + 1])[None, :]

    def trig(n):
        ang = ((n[:, None] * k) % l).astype(F32) * (2.0 * math.pi / l)
        return jnp.cos(ang), jnp.sin(ang)

    ch, sh = trig(jnp.arange(l // 2 // r, dtype=jnp.int32) * r)
    cl, sl = trig(jnp.arange(r, dtype=jnp.int32))
    scale = l ** -0.5
    cos = (ch[:, None, :] * cl[None, :, :] - sh[:, None, :] * sl[None, :, :]).reshape(l // 2, l)
    sin = (sh[:, None, :] * cl[None, :, :] + ch[:, None, :] * sl[None, :, :]).reshape(l // 2, l)
    return (cos * scale).astype(BF16), (sin * scale).astype(BF16)


def _channel_dft_tables():
    n = jnp.arange(256, dtype=jnp.int32)
    same = (n[:, None] // HEAD_DIM) == (n[None, :] // HEAD_DIM)
    ang = (((n[:, None] % HEAD_DIM) * (n[None, :] % HEAD_DIM)) % HEAD_DIM).astype(F32) * (2.0 * math.pi / HEAD_DIM)
    scale = HEAD_DIM ** -0.5
    cc = jnp.where(same, jnp.cos(ang) * scale, 0.0).astype(BF16)
    sc = jnp.where(same, jnp.sin(ang) * scale, 0.0).astype(BF16)
    return cc, sc


def _group_matrix(width):
    n = jnp.arange(width, dtype=jnp.int32)
    return ((n[:, None] // HEAD_DIM) == (n[None, :] // HEAD_DIM)).astype(BF16)


def _rope_tables(n_tokens):
    t = jnp.arange(n_tokens, dtype=jnp.int32)
    row = (t // GRID_W).astype(F32)
    col = (t % GRID_W).astype(F32)
    n_freq = HEAD_DIM // 4
    inv_freq = ROPE_THETA ** (-jnp.arange(n_freq, dtype=F32) / n_freq)
    ang = jnp.concatenate([row[:, None] * inv_freq, col[:, None] * inv_freq], axis=-1)
    cos, sin = jnp.cos(ang), jnp.sin(ang)
    cos_t = jnp.concatenate([cos, cos, cos, cos], axis=-1)
    sin_t = jnp.concatenate([-sin, sin, -sin, sin], axis=-1)
    return cos_t, sin_t


def _gla_tables():
    c = GLA_CHUNK
    i = jnp.arange(c, dtype=jnp.int32)[:, None]
    t = jnp.arange(c, dtype=jnp.int32)[None, :]
    ones_row = jnp.ones((16, c), jnp.bool_)

    def tables(forward):
        z = [t <= i, t > i] if forward else [t >= i, t < i]
        m = []
        for s in GLA_LEVELS:
            pos = i % (2 * s)
            base = i - pos
            bound = base + s - 1
            right = pos >= s
            if forward:
                zq = right & (t > bound) & (t <= i)
                zk = (~right) & (t > i) & (t <= bound)
            else:
                zq = (~right) & (t >= i) & (t <= bound)
                zk = right & (t > bound) & (t < i)
            z.append(zq | zk)
            same_parent = (i // (2 * s)) == (t // (2 * s))
            t_right = (t % (2 * s)) >= s
            if forward:
                m.append(same_parent & right & (~t_right))
            else:
                m.append(same_parent & (~right) & t_right)
        m.append(i == t)
        z.append(ones_row)
        zt = jnp.concatenate(z, axis=0).astype(BF16)
        m.append(jnp.zeros_like(m[0]))
        mt = jnp.stack([jnp.tile(jnp.concatenate([m[2 * p], m[2 * p + 1]], axis=1), (4, 1))
                        for p in range(len(m) // 2)]).astype(F32)
        return zt, mt

    zf, mf = tables(True)
    zb, mb = tables(False)
    return jnp.stack([zf, zb]), jnp.stack([mf, mb])


def _na_bias_table(rpb):
    c = jnp.arange(GRID_W)[:, None, None]
    kc = jnp.arange(GRID_W)[None, :, None]
    m = jnp.arange(2 * NA_COLS - 1)[None, None, :]
    cs = jnp.clip(c - NA_COLS // 2, 0, GRID_W - NA_COLS)
    valid = (kc >= cs) & (kc < cs + NA_COLS)
    pick = (valid & (kc - c + NA_COLS - 1 == m)).astype(F32)
    by_row = jnp.einsum('hrm,ckm->rhck', rpb.astype(F32), pick, precision=lax.Precision.HIGHEST)
    by_row = jnp.where(valid[None, None, :, :, 0], by_row, NEG)
    blocks = [jnp.concatenate([by_row[j - off + NA_ROWS - 1] for j in range(NA_ROWS)], axis=-1)
              for off in range(NA_ROWS)]
    return jnp.stack(blocks).reshape(NA_ROWS, 4 * GRID_W, NA_ROWS * GRID_W)


_GQA_PERM = (0, 2, 1, 3)


def _layer_weights(l, w_in, na_q_norm, na_k_norm, gla_w_gate_f, gla_b_gate_f, gla_w_gate_b, gla_b_gate_b,
                   gla_norm, gqa_q_norm, gqa_k_norm, w_out):
    wi = w_in[l]
    d = wi.shape[0]
    gq = wi[:, 1824:2080].reshape(d, 4, HEAD_DIM)[:, _GQA_PERM, :].reshape(d, 256)
    w_in_r = jnp.concatenate([wi[:, 0:1792], gq, wi[:, 2080:2336], wi[:, 1792:1824],
                              jnp.zeros((d, 96), F32)], axis=1).astype(BF16)
    wg = jnp.zeros((128, 256), F32)
    wg = wg.at[0:16, 0:128].set(gla_w_gate_f[l]).at[16:32, 128:256].set(gla_w_gate_b[l]).astype(BF16)
    bg = jnp.concatenate([gla_b_gate_f[l], gla_b_gate_b[l]])[None, :]
    wo = w_out[l]
    wo_d = wo[768:1024].reshape(4, HEAD_DIM, -1)[_GQA_PERM, :, :].reshape(256, -1)
    w_out_r = jnp.concatenate([wo[0:768], wo_d], axis=0).astype(BF16)

    def tile4(g):
        return jnp.tile(g, 4)[None, :]

    return {
        "w_in": w_in_r, "wg": wg, "bg": bg, "w_out": w_out_r,
        "naq_g": tile4(na_q_norm[l]), "nak_g": tile4(na_k_norm[l]),
        "gq_g": tile4(gqa_q_norm[l]), "gk_g": tile4(gqa_k_norm[l]),
        "gn": tile4(gla_norm[l]),
    }


def kernel(x, c, ctx, c_ctx, w_mod, b_mod, ffn1_w1, ffn1_w3, ffn1_w2, w_in, na_q_norm, na_k_norm, na_rpb,
           gla_w_gate_f, gla_b_gate_f, gla_w_gate_b, gla_b_gate_b, gla_norm, gqa_q_norm, gqa_k_norm, w_out,
           ffn2_w1, ffn2_w3, ffn2_w2):
    nb, s, d = x.shape
    lc = ctx.shape[1]
    depth = w_mod.shape[0]
    tm_x = 512
    tm_z = 256

    cc = jnp.zeros((16, d), F32).at[0:nb].set(c).at[nb].set(c_ctx)
    mod_all = _modulation(cc, w_mod, b_mod)

    consts = {"g256": _group_matrix(256), "g128": _group_matrix(128)}
    consts["cc"], consts["sc"] = _channel_dft_tables()
    cos_t, sin_t = _rope_tables(s)
    dft_x = _dft_tables(s)
    dft_z = _dft_tables(lc)
    z_tab, gla_masks = _gla_tables()
    s_zero = jnp.zeros((nb, 2, 256, 128), F32)

    z = ctx.reshape(1, nb * lc, d)
    four_x = (lambda b, t: (0, t, b), (2, s // 2, nb * 256))
    four_z = (lambda b, t: (0, 0, t), (2, lc // 2, nb * 256))

    def mix_x(b, t):
        return (t, b)

    def mix_z(b, t):
        return (0, t)

    for l in range(depth):
        last = l == depth - 1
        mod = mod_all[l, 0:nb + 1].reshape(nb + 1, N_MOD, d)
        mod_x, mod_z = mod[0:nb], mod[nb:nb + 1]
        lw = _layer_weights(l, w_in, na_q_norm, na_k_norm, gla_w_gate_f, gla_b_gate_f, gla_w_gate_b,
                            gla_b_gate_b, gla_norm, gqa_q_norm, gqa_k_norm, w_out)
        f1 = (ffn1_w1[l].astype(BF16), ffn1_w3[l].astype(BF16), ffn1_w2[l].astype(BF16))
        f2 = (ffn2_w1, ffn2_w3, ffn2_w2)
        tb = _na_bias_table(na_rpb[l])

        px = _ffn_inproj(x, mod_x, f1, lw, consts, cos_t, sin_t, True, tm_x, four_x)
        pz = _ffn_inproj(z, mod_z, f1, lw, consts, cos_t, sin_t, False, tm_z, four_z)
        (x, ucx, usx, naq_x, nak_x, nav_x, qka_x, gv_x, gr_x, q_x, k_x, v_x) = px
        z = pz[0]
        (ucz, usz, naq_z, nak_z, nav_z, qka_z, gv_z, gr_z, q_z, k_z, v_z) = [
            t if t.shape[0] == 2 else t.reshape(nb, lc, t.shape[-1]) for t in pz[1:]]

        a_x = _dft(dft_x[0], dft_x[1], ucx, usx, tm=256)
        b_x = _na(naq_x, nak_x, nav_x, nak_z, nav_z, tb)
        ofz, obz, s_ctx = _gla(qka_z, gv_z, s_zero, z_tab, gla_masks, chunks=lc // GLA_CHUNK)
        ofx, obx, _ = _gla(qka_x, gv_x, s_ctx, z_tab, gla_masks, chunks=8)
        d_x = _gqa(q_x, k_x, v_x, k_z, v_z, tq=256, kb=2048)

        x = _outproj(x, mod_x, a_x, mix_x, b_x, ofx, obx, gr_x, d_x, lw["w_out"], consts["g256"],
                     lw["gn"], tm_x)
        x = _ffn(x, mod_x, f2, l, row0=6, tm=tm_x)

        if not last:
            a_z = _dft(dft_z[0], dft_z[1], ucz, usz, tm=lc // 2)
            b_z, d_z = _ctx_attn(naq_z, nak_z, nav_z, q_z, k_z, v_z)
            z = _outproj(z, mod_z, a_z, mix_z, b_z.reshape(1, nb * lc, 256),
                         ofz.reshape(1, nb * lc, 256), obz.reshape(1, nb * lc, 256),
                         gr_z.reshape(1, nb * lc, 256), d_z.reshape(1, nb * lc, 256),
                         lw["w_out"], consts["g256"], lw["gn"], tm_z)
            z = _ffn(z, mod_z, f2, l, row0=6, tm=tm_z)
    return x
```

```python
import functools
import math

import jax
import jax.numpy as jnp
from jax import lax
from jax.experimental import pallas as pl
from jax.experimental.pallas import tpu as pltpu

F32 = jnp.float32
BF16 = jnp.bfloat16

HEAD_DIM = 64
GRID_W = 64
NA_ROWS = 8
NA_COLS = 16
GLA_DK = 32
GLA_TAU = 16.0
GLA_CHUNK = 64
GLA_LEVELS = (32, 16, 8, 4, 2, 1)
ROPE_THETA = 10000.0
EPS = 1e-6
N_MOD = 9
NEG = -1e30

VMEM_LIMIT = 56 * 1024 * 1024

TILES = {
    "rows_latent": 512,
    "rows_context": 256,
    "ffn_chunk": 512,
    "mod_cols": 1024,
    "dft_rows": 256,
    "na_rows": 16,
    "gqa_queries": 256,
    "gqa_keys": 2048,
    "softmax_rows": 16,
    "gla_chunks": 8,
}


def _cparams(*sem):
    return pltpu.CompilerParams(dimension_semantics=sem, vmem_limit_bytes=VMEM_LIMIT)


def _dot(a, b):
    return jnp.dot(a, b, preferred_element_type=F32)


def _dot_nt(a, b):
    return lax.dot_general(a, b, (((1,), (1,)), ((), ())), preferred_element_type=F32)


def _dot_tn(a, b):
    return lax.dot_general(a, b, (((0,), (0,)), ((), ())), preferred_element_type=F32)


def _sigmoid(x):
    return 1.0 / (1.0 + jnp.exp(-x))


def _silu(x):
    return x * _sigmoid(x)


def _log_sigmoid(x):
    return jnp.minimum(x, 0.0) - jnp.log(1.0 + jnp.exp(-jnp.abs(x)))


def _const_spec(shape):
    nd = len(shape)
    return pl.BlockSpec(shape, lambda *_: (0,) * nd, pipeline_mode=pl.Buffered(1))


def _mod_kernel(c_ref, w_ref, b_ref, o_ref):
    h = _silu(c_ref[...]).astype(BF16)
    o_ref[0] = _dot(h, w_ref[0].astype(BF16)) + b_ref[0]


def _modulation(cc, w_mod, b_mod):
    depth, d, n = w_mod.shape
    rows = cc.shape[0]
    tn = TILES["mod_cols"]
    return pl.pallas_call(
        _mod_kernel,
        out_shape=jax.ShapeDtypeStruct((depth, rows, n), F32),
        grid=(depth, n // tn),
        in_specs=[
            pl.BlockSpec((rows, d), lambda l, j: (0, 0)),
            pl.BlockSpec((1, d, tn), lambda l, j: (l, 0, j)),
            pl.BlockSpec((1, 1, tn), lambda l, j: (l, 0, j)),
        ],
        out_specs=pl.BlockSpec((1, rows, tn), lambda l, j: (l, 0, j)),
        compiler_params=_cparams("parallel", "parallel"),
        name="adaln_mod",
    )(cc, w_mod, b_mod.reshape(depth, 1, n))


def _norm_mod(x, shift, scale):
    h = x * lax.rsqrt(jnp.mean(x * x, axis=-1, keepdims=True) + EPS)
    return h * (1.0 + scale) + shift


def _swiglu_acc(hb, w1_ref, w3_ref, w2_ref, fc):
    d_ff = w1_ref.shape[1]
    acc = jnp.zeros((hb.shape[0], w2_ref.shape[1]), F32)
    for f0 in range(0, d_ff, fc):
        a = _dot(hb, w1_ref[:, f0:f0 + fc].astype(BF16))
        b = _dot(hb, w3_ref[:, f0:f0 + fc].astype(BF16))
        g = (_silu(a) * b).astype(BF16)
        acc = acc + _dot(g, w2_ref[f0:f0 + fc, :].astype(BF16))
    return acc


def _group_sum(xsq, gmat):
    hi = xsq.astype(BF16)
    lo = (xsq - hi.astype(F32)).astype(BF16)
    return _dot(hi, gmat) + _dot(lo, gmat)


def _head_rms(x, gmat, gain):
    ms = _group_sum(x * x, gmat) * (1.0 / HEAD_DIM)
    return x * lax.rsqrt(ms + EPS) * gain


def _rope128(x, cos, sin_signed):
    lane = lax.broadcasted_iota(jnp.int32, x.shape, 1)
    first_half = (lane % HEAD_DIM) < (HEAD_DIM // 2)
    partner = jnp.where(first_half, pltpu.roll(x, 96, 1), pltpu.roll(x, 32, 1))
    return x * cos + partner * sin_signed


def _inproj_kernel(x_ref, mod_ref, w1_ref, w3_ref, w2_ref, w_ref, g256_ref, g128_ref, cc_ref, sc_ref,
                   naq_g_ref, nak_g_ref, gq_g_ref, gk_g_ref, wg_ref, bg_ref, cos_ref, sin_ref,
                   x1_ref, uc_ref, us_ref, naq_ref, nak_ref, nav_ref, qka_ref, gv_ref, gr_ref,
                   q_ref, k_ref, v_ref, four_ref, *, rope, fc):
    x0 = x_ref[0]
    hb0 = _norm_mod(x0, mod_ref[0, 0:1, :], mod_ref[0, 1:2, :]).astype(BF16)
    x = x0 + (0.5 * mod_ref[0, 2:3, :]) * _swiglu_acc(hb0, w1_ref, w3_ref, w2_ref, fc)
    x1_ref[0] = x
    shift = mod_ref[0, 3:4, :]
    scale = mod_ref[0, 4:5, :]
    hb = _norm_mod(x, shift, scale).astype(BF16)
    h = _dot(hb, w_ref[...])
    g256 = g256_ref[...]
    g128 = g128_ref[...]
    ub = h[:, 0:256].astype(BF16)
    half = four_ref.shape[1] // 2
    for w, (out_ref, tab_ref) in enumerate(((uc_ref, cc_ref), (us_ref, sc_ref))):
        u = _dot(ub, tab_ref[...])
        for j in range(2):
            s = 2 * w + j
            four_ref[s] = u[:, j * 128:(j + 1) * 128]
            out_ref[0, :, j * 128:(j + 1) * 128] = four_ref[s, pl.ds(0, half, stride=2), :].astype(BF16)
            out_ref[1, :, j * 128:(j + 1) * 128] = four_ref[s, pl.ds(1, half, stride=2), :].astype(BF16)
    naq_ref[0] = (_head_rms(h[:, 256:512], g256, naq_g_ref[...]) * HEAD_DIM ** -0.5).astype(BF16)
    nak_ref[0] = _head_rms(h[:, 512:768], g256, nak_g_ref[...]).astype(BF16)
    nav_ref[0] = h[:, 768:1024].astype(BF16)
    gates = _dot(h[:, 2304:2432].astype(BF16), wg_ref[...]) + bg_ref[...]
    log_a = _log_sigmoid(gates) / GLA_TAU
    qka_ref[0, :, 0:128] = h[:, 1024:1152] * GLA_DK ** -0.5
    qka_ref[0, :, 128:256] = h[:, 1152:1280]
    qka_ref[0, :, 256:512] = log_a
    gv_ref[0] = h[:, 1280:1536].astype(BF16)
    gr_ref[0] = _silu(h[:, 1536:1792]).astype(BF16)
    q = _head_rms(h[:, 1792:2048], g256, gq_g_ref[...])
    k = _head_rms(h[:, 2048:2176], g128, gk_g_ref[:, 0:128])
    if rope:
        cos = cos_ref[...]
        sin = sin_ref[...]
        q = jnp.concatenate([_rope128(q[:, 0:128], cos, sin), _rope128(q[:, 128:256], cos, sin)], axis=1)
        k = _rope128(k, cos, sin)
    q_ref[0] = (q * HEAD_DIM ** -0.5).astype(BF16)
    k_ref[0] = k.astype(BF16)
    v_ref[0] = h[:, 2176:2304].astype(BF16)


def _ffn_inproj(x, mod, ffn_w, lw, consts, cos_t, sin_t, rope, tm, four_index):
    nb, r, d = x.shape
    n_four = four_index[1]
    w1, w3, w2 = ffn_w

    def act(width, dtype):
        return jax.ShapeDtypeStruct((nb, r, width), dtype)

    def act_spec(width):
        return pl.BlockSpec((1, tm, width), lambda b, t: (b, t, 0))

    four_shape = jax.ShapeDtypeStruct(n_four, BF16)
    four_spec = pl.BlockSpec((2, tm // 2, 256), four_index[0])
    vec256 = _const_spec((1, 256))
    out_shapes = [act(d, F32), four_shape, four_shape, act(256, BF16), act(256, BF16), act(256, BF16),
                  act(512, F32), act(256, BF16), act(256, BF16),
                  act(256, BF16), act(128, BF16), act(128, BF16)]
    out_specs = [act_spec(d), four_spec, four_spec, act_spec(256), act_spec(256), act_spec(256),
                 act_spec(512), act_spec(256), act_spec(256),
                 act_spec(256), act_spec(128), act_spec(128)]
    return pl.pallas_call(
        functools.partial(_inproj_kernel, rope=rope, fc=TILES["ffn_chunk"]),
        out_shape=out_shapes,
        grid=(nb, r // tm),
        in_specs=[
            pl.BlockSpec((1, tm, d), lambda b, t: (b, t, 0)),
            pl.BlockSpec((1, N_MOD, d), lambda b, t: (b, 0, 0)),
            _const_spec(w1.shape), _const_spec(w3.shape), _const_spec(w2.shape),
            _const_spec(lw["w_in"].shape),
            _const_spec((256, 256)), _const_spec((128, 128)),
            _const_spec((256, 256)), _const_spec((256, 256)),
            vec256, vec256, vec256, vec256,
            _const_spec((128, 256)), vec256,
            pl.BlockSpec((tm, 128), lambda b, t: (t, 0)),
            pl.BlockSpec((tm, 128), lambda b, t: (t, 0)),
        ],
        out_specs=out_specs,
        scratch_shapes=[pltpu.VMEM((4, tm, 128), F32)],
        compiler_params=_cparams("parallel", "parallel"),
        name="ffn1_in_proj",
    )(x, mod, w1, w3, w2, lw["w_in"], consts["g256"], consts["g128"], consts["cc"], consts["sc"],
      lw["naq_g"], lw["nak_g"], lw["gq_g"], lw["gk_g"], lw["wg"], lw["bg"], cos_t, sin_t)


def _dft_kernel(c_ref, s_ref, uc_ref, us_ref, lo_ref, hi_ref):
    h = uc_ref.shape[1]
    a_e = _dot(c_ref[:, 0:h], uc_ref[0]) - _dot(s_ref[:, 0:h], us_ref[0])
    a_o = _dot(c_ref[:, h:2 * h], uc_ref[1]) - _dot(s_ref[:, h:2 * h], us_ref[1])
    lo_ref[...] = (a_e + a_o).astype(lo_ref.dtype)
    hi_ref[...] = (a_e - a_o).astype(hi_ref.dtype)


def _dft(c_tab, s_tab, uc, us, tm):
    _, h, n = uc.shape
    half = jax.ShapeDtypeStruct((h, n), BF16)
    lo, hi = pl.pallas_call(
        _dft_kernel,
        out_shape=[half, half],
        grid=(h // tm,),
        in_specs=[
            pl.BlockSpec((tm, 2 * h), lambda i: (i, 0)),
            pl.BlockSpec((tm, 2 * h), lambda i: (i, 0)),
            _const_spec(uc.shape),
            _const_spec(us.shape),
        ],
        out_specs=[pl.BlockSpec((tm, n), lambda i: (i, 0)), pl.BlockSpec((tm, n), lambda i: (i, 0))],
        compiler_params=_cparams("parallel"),
        name="position_dft",
    )(c_tab, s_tab, uc, us)
    return jnp.concatenate([lo, hi], axis=0)


def _stack_heads(q, n_heads, width):
    lane = lax.broadcasted_iota(jnp.int32, q.shape, 1)
    zero = jnp.zeros_like(q)
    return jnp.concatenate([jnp.where(lane // width == h, q, zero) for h in range(n_heads)], axis=0)


def _unstack_heads(o, n_heads, width):
    m = o.shape[0] // n_heads
    lane = lax.broadcasted_iota(jnp.int32, (m, o.shape[1]), 1)
    out = jnp.zeros((m, o.shape[1]), F32)
    for h in range(n_heads):
        out = out + jnp.where(lane // width == h, o[h * m:(h + 1) * m, :], 0.0)
    return out


def _softmax_pv(score_blocks, value_blocks):
    m = score_blocks[0].max(axis=-1, keepdims=True)
    for s in score_blocks[1:]:
        m = jnp.maximum(m, s.max(axis=-1, keepdims=True))
    l = None
    o = None
    for s, v in zip(score_blocks, value_blocks):
        p = jnp.exp(s - m)
        ps = p.sum(axis=-1, keepdims=True)
        pv = _dot(p.astype(BF16), v)
        l = ps if l is None else l + ps
        o = pv if o is None else o + pv
    return o * (1.0 / l)


def _na_kernel(q_ref, k_ref, v_ref, kc_ref, vc_ref, tb_ref, o_ref, *, rows_per_step, n_rows):
    rb = pl.program_id(1)
    kc = kc_ref[0]
    vc = vc_ref[0]
    for i in range(rows_per_step):
        r = rb * rows_per_step + i
        rs = jnp.clip(r - NA_ROWS // 2, 0, n_rows - NA_ROWS)
        off = r - rs
        q = q_ref[0, i * GRID_W:(i + 1) * GRID_W, :]
        qs = _stack_heads(q, 4, HEAD_DIM)
        start = pl.multiple_of(rs * GRID_W, GRID_W)
        kw = k_ref[0, pl.ds(start, NA_ROWS * GRID_W), :]
        vw = v_ref[0, pl.ds(start, NA_ROWS * GRID_W), :]
        s_w = _dot_nt(qs, kw) + tb_ref[off]
        s_c = _dot_nt(qs, kc)
        o = _softmax_pv([s_w, s_c], [vw, vc])
        o_ref[0, i * GRID_W:(i + 1) * GRID_W, :] = _unstack_heads(o, 4, HEAD_DIM).astype(BF16)


def _na(q, k, v, kc, vc, tb):
    nb, s, w = q.shape
    n_rows = s // GRID_W
    rps = TILES["na_rows"]
    lc = kc.shape[1]
    return pl.pallas_call(
        functools.partial(_na_kernel, rows_per_step=rps, n_rows=n_rows),
        out_shape=jax.ShapeDtypeStruct((nb, s, w), BF16),
        grid=(nb, n_rows // rps),
        in_specs=[
            pl.BlockSpec((1, rps * GRID_W, w), lambda b, t: (b, t, 0)),
            pl.BlockSpec((1, s, w), lambda b, t: (b, 0, 0)),
            pl.BlockSpec((1, s, w), lambda b, t: (b, 0, 0)),
            pl.BlockSpec((1, lc, w), lambda b, t: (b, 0, 0)),
            pl.BlockSpec((1, lc, w), lambda b, t: (b, 0, 0)),
            _const_spec(tb.shape),
        ],
        out_specs=pl.BlockSpec((1, rps * GRID_W, w), lambda b, t: (b, t, 0)),
        compiler_params=_cparams("parallel", "parallel"),
        name="neighborhood_attn",
    )(q, k, v, kc, vc, tb)


def _gqa_scores_lhs(q):
    return jnp.concatenate([_stack_heads(q[:, 0:128], 2, HEAD_DIM),
                            _stack_heads(q[:, 128:256], 2, HEAD_DIM)], axis=0)


def _gqa_merge(o):
    m = o.shape[0] // 4
    return jnp.concatenate([_unstack_heads(o[0:2 * m], 2, HEAD_DIM),
                            _unstack_heads(o[2 * m:4 * m], 2, HEAD_DIM)], axis=1)


def _gqa_kernel(q_ref, k_ref, v_ref, kc_ref, vc_ref, o_ref, s_ref, p_ref, l_ref, *, kb, rb):
    qs = _gqa_scores_lhs(q_ref[0])
    n = qs.shape[0]
    m = jnp.full((n, 1), NEG, F32)
    acc = jnp.zeros((n, 128), F32)
    l_ref[...] = jnp.zeros_like(l_ref)
    blocks = [(k_ref, v_ref, j * kb, kb) for j in range(k_ref.shape[1] // kb)]
    blocks.append((kc_ref, vc_ref, 0, kc_ref.shape[1]))
    for kr, vr, k0, size in blocks:
        s = _dot_nt(qs, kr[0, k0:k0 + size, :])
        s_ref[:, 0:size] = s
        m_new = jnp.maximum(m, s.max(axis=-1, keepdims=True))
        alpha = jnp.exp(m - m_new)
        for r0 in range(0, n, rb):
            p = jnp.exp(s_ref[r0:r0 + rb, 0:size] - m_new[r0:r0 + rb])
            part = p[:, 0:128]
            for j in range(1, size // 128):
                part = part + p[:, j * 128:(j + 1) * 128]
            l_ref[r0:r0 + rb, :] = l_ref[r0:r0 + rb, :] * alpha[r0:r0 + rb] + part
            p_ref[r0:r0 + rb, 0:size] = p.astype(BF16)
        acc = acc * alpha + _dot(p_ref[:, 0:size], vr[0, k0:k0 + size, :])
        m = m_new
    l = l_ref[...].sum(axis=-1, keepdims=True)
    o_ref[0] = _gqa_merge(acc * (1.0 / l)).astype(BF16)


def _gqa(q, k, v, kc, vc, tq, kb):
    nb, s, _ = q.shape
    lc = kc.shape[1]
    return pl.pallas_call(
        functools.partial(_gqa_kernel, kb=kb, rb=TILES["softmax_rows"]),
        out_shape=jax.ShapeDtypeStruct((nb, s, 256), BF16),
        grid=(nb, s // tq),
        in_specs=[
            pl.BlockSpec((1, tq, 256), lambda b, t: (b, t, 0)),
            pl.BlockSpec((1, s, 128), lambda b, t: (b, 0, 0)),
            pl.BlockSpec((1, s, 128), lambda b, t: (b, 0, 0)),
            pl.BlockSpec((1, lc, 128), lambda b, t: (b, 0, 0)),
            pl.BlockSpec((1, lc, 128), lambda b, t: (b, 0, 0)),
        ],
        out_specs=pl.BlockSpec((1, tq, 256), lambda b, t: (b, t, 0)),
        scratch_shapes=[pltpu.VMEM((4 * tq, kb), F32), pltpu.VMEM((4 * tq, kb), BF16),
                        pltpu.VMEM((4 * tq, 128), F32)],
        compiler_params=_cparams("parallel", "parallel"),
        name="gqa_attn",
    )(q, k, v, kc, vc)


def _ctx_attn_kernel(naq_ref, nak_ref, nav_ref, q_ref, k_ref, v_ref, bz_ref, dz_ref):
    qs = _stack_heads(naq_ref[0], 4, HEAD_DIM)
    o = _softmax_pv([_dot_nt(qs, nak_ref[0])], [nav_ref[0]])
    bz_ref[0] = _unstack_heads(o, 4, HEAD_DIM).astype(BF16)
    qg = _gqa_scores_lhs(q_ref[0])
    og = _softmax_pv([_dot_nt(qg, k_ref[0])], [v_ref[0]])
    dz_ref[0] = _gqa_merge(og).astype(BF16)


def _ctx_attn(naq, nak, nav, q, k, v):
    nb, lc, _ = naq.shape

    def spec(w):
        return pl.BlockSpec((1, lc, w), lambda b: (b, 0, 0))

    out = jax.ShapeDtypeStruct((nb, lc, 256), BF16)
    return pl.pallas_call(
        _ctx_attn_kernel,
        out_shape=[out, out],
        grid=(nb,),
        in_specs=[spec(256), spec(256), spec(256), spec(256), spec(128), spec(128)],
        out_specs=[spec(256), spec(256)],
        compiler_params=_cparams("parallel"),
        name="context_attn",
    )(naq, nak, nav, q, k, v)


def _gla_kernel(qf_ref, vf_ref, qb_ref, vb_ref, z_ref, mask_ref, s0_ref, of_ref, ob_ref, sfin_ref,
                st_ref, *, chunks):
    p = pl.program_id(1)
    c = GLA_CHUNK

    @pl.when(p == 0)
    def _():
        st_ref[...] = s0_ref[0]

    row = lax.broadcasted_iota(jnp.int32, (256, 128), 0)
    lane = lax.broadcasted_iota(jnp.int32, (256, 128), 1)
    bd_mask = (row // HEAD_DIM) == (lane // GLA_DK)
    row4 = lax.broadcasted_iota(jnp.int32, (4 * c, 128), 0)
    lane4 = lax.broadcasted_iota(jnp.int32, (4 * c, 128), 1)
    head_mask = (row4 // c) == (lane4 // GLA_DK)
    n_lev = len(GLA_LEVELS)
    zq = jnp.zeros((4 * c, 128), BF16)
    zk = jnp.zeros((c, 128), BF16)

    units = []
    for ci in range(chunks):
        units.append((0, qf_ref, vf_ref, of_ref, ci * c, 256))
        units.append((1, qb_ref, vb_ref, ob_ref, (chunks - 1 - ci) * c, 384))

    loaded = []
    for d, in_ref, v_ref, o_ref, r0, a0 in units:
        q = in_ref[0, r0:r0 + c, 0:128]
        k = in_ref[0, r0:r0 + c, 128:256]
        a = in_ref[0, r0:r0 + c, a0:a0 + 128]
        v = v_ref[0, r0:r0 + c, :]
        a_hi = a.astype(BF16)
        a_lo = (a - a_hi.astype(F32)).astype(BF16)
        e2 = _dot(z_ref[d], jnp.concatenate([a_hi, a_lo], axis=1))
        x = jnp.exp(e2[:, 0:128] + e2[:, 128:256])
        loaded.append((q, k, v, x))

    st = [st_ref[0], st_ref[1]]
    for (d, _, _, o_ref, r0, _), (q, k, v, x) in zip(units, loaded):

        def level_operands(lev):
            if lev < n_lev:
                xl = x[(2 + lev) * c:(3 + lev) * c]
                ql, kl = q * xl, k * xl
            else:
                ql, kl = q, k
            qs = jnp.where(head_mask, jnp.concatenate([ql, ql, ql, ql], axis=0), 0.0)
            return qs.astype(BF16), kl.astype(BF16)

        scores = jnp.zeros((4 * c, 2 * c), F32)
        for pair in range((n_lev + 2) // 2):
            qa, ka = level_operands(2 * pair)
            qb, kb = level_operands(2 * pair + 1) if 2 * pair + 1 <= n_lev else (zq, zk)
            lhs = jnp.concatenate([qa, qb], axis=1)
            rhs = jnp.concatenate([jnp.concatenate([ka, zk], axis=1),
                                   jnp.concatenate([zk, kb], axis=1)], axis=0)
            scores = scores + _dot_nt(lhs, rhs) * mask_ref[d, pair]
        o_all = _dot(scores.astype(BF16), jnp.concatenate([v, v], axis=0))
        o_intra = _unstack_heads(o_all, 4, HEAD_DIM)
        qe = (q * x[0:c]).astype(BF16)
        ke = (k * x[c:2 * c]).astype(BF16)
        upd = jnp.where(bd_mask, _dot_tn(v, ke), 0.0)
        decay = x[8 * c:8 * c + 1, :]
        o_ref[0, r0:r0 + c, :] = (o_intra + _dot_nt(qe, st[d].astype(BF16))).astype(BF16)
        st[d] = st[d] * decay + upd
    st_ref[0] = st[0]
    st_ref[1] = st[1]

    @pl.when(p == pl.num_programs(1) - 1)
    def _():
        sfin_ref[0] = st_ref[...]


def _gla(qka, v, s0, z_tab, masks, chunks):
    nb, r, _ = qka.shape
    blk = chunks * GLA_CHUNK
    nblk = r // blk
    out = jax.ShapeDtypeStruct((nb, r, 256), BF16)
    return pl.pallas_call(
        functools.partial(_gla_kernel, chunks=chunks),
        out_shape=[out, out, jax.ShapeDtypeStruct((nb, 2, 256, 128), F32)],
        grid=(nb, nblk),
        in_specs=[
            pl.BlockSpec((1, blk, 512), lambda b, p: (b, p, 0)),
            pl.BlockSpec((1, blk, 256), lambda b, p: (b, p, 0)),
            pl.BlockSpec((1, blk, 512), lambda b, p: (b, nblk - 1 - p, 0)),
            pl.BlockSpec((1, blk, 256), lambda b, p: (b, nblk - 1 - p, 0)),
            _const_spec(z_tab.shape),
            _const_spec(masks.shape),
            pl.BlockSpec((1, 2, 256, 128), lambda b, p: (b, 0, 0, 0)),
        ],
        out_specs=[
            pl.BlockSpec((1, blk, 256), lambda b, p: (b, p, 0)),
            pl.BlockSpec((1, blk, 256), lambda b, p: (b, nblk - 1 - p, 0)),
            pl.BlockSpec((1, 2, 256, 128), lambda b, p: (b, 0, 0, 0)),
        ],
        scratch_shapes=[pltpu.VMEM((2, 256, 128), F32)],
        compiler_params=_cparams("parallel", "arbitrary"),
        name="gla_scan",
    )(qka, v, qka, v, z_tab, masks, s0)


def _outproj_kernel(x_ref, mod_ref, a_ref, b_ref, of_ref, ob_ref, r_ref, d_ref, w_ref, g256_ref,
                    gn_ref, o_ref):
    o = of_ref[0].astype(F32) + ob_ref[0].astype(F32)
    cx = (_head_rms(o, g256_ref[...], gn_ref[...]) * r_ref[0].astype(F32)).astype(BF16)
    acc = _dot(a_ref[...], w_ref[0:256, :])
    acc = acc + _dot(b_ref[0], w_ref[256:512, :])
    acc = acc + _dot(cx, w_ref[512:768, :])
    acc = acc + _dot(d_ref[0], w_ref[768:1024, :])
    o_ref[0] = x_ref[0] + mod_ref[0, 5:6, :] * acc


def _outproj(x, mod, a, a_index, b, o_f, o_b, r, dd, w_out, g256, gn, tm):
    nb, rows, d = x.shape

    def act_spec(width):
        return pl.BlockSpec((1, tm, width), lambda bb, t: (bb, t, 0))

    return pl.pallas_call(
        _outproj_kernel,
        out_shape=jax.ShapeDtypeStruct(x.shape, F32),
        grid=(nb, rows // tm),
        in_specs=[
            act_spec(d),
            pl.BlockSpec((1, N_MOD, d), lambda bb, t: (bb, 0, 0)),
            pl.BlockSpec((tm, 256), a_index),
            act_spec(256), act_spec(256), act_spec(256), act_spec(256), act_spec(256),
            _const_spec(w_out.shape), _const_spec((256, 256)), _const_spec((1, 256)),
        ],
        out_specs=act_spec(d),
        compiler_params=_cparams("parallel", "parallel"),
        name="out_proj",
    )(x, mod, a, b, o_f, o_b, r, dd, w_out, g256, gn)


def _ffn_kernel(x_ref, mod_ref, w1_ref, w3_ref, w2_ref, o_ref, *, row0, fc):
    x = x_ref[0]
    hb = _norm_mod(x, mod_ref[0, row0:row0 + 1, :], mod_ref[0, row0 + 1:row0 + 2, :]).astype(BF16)
    o_ref[0] = x + (0.5 * mod_ref[0, row0 + 2:row0 + 3, :]) * _swiglu_acc(hb, w1_ref, w3_ref, w2_ref, fc)


def _ffn(x, mod, ffn_w, layer, row0, tm):
    nb, r, d = x.shape
    w1, w3, w2 = ffn_w

    def layer_spec(w):
        return pl.BlockSpec((None,) + w.shape[1:], lambda b, t: (layer, 0, 0), pipeline_mode=pl.Buffered(1))

    return pl.pallas_call(
        functools.partial(_ffn_kernel, row0=row0, fc=TILES["ffn_chunk"]),
        out_shape=jax.ShapeDtypeStruct(x.shape, F32),
        grid=(nb, r // tm),
        in_specs=[
            pl.BlockSpec((1, tm, d), lambda b, t: (b, t, 0)),
            pl.BlockSpec((1, N_MOD, d), lambda b, t: (b, 0, 0)),
            layer_spec(w1), layer_spec(w3), layer_spec(w2),
        ],
        out_specs=pl.BlockSpec((1, tm, d), lambda b, t: (b, t, 0)),
        compiler_params=_cparams("parallel", "parallel"),
        name="ffn_halfstep",
    )(x, mod, w1, w3, w2)


def _dft_tables(l):
    r = 1 << (int(math.log2(l)) // 2)
    j = jnp.arange(l // 2, dtype=jnp.int32)
    k = jnp.concatenate([2 * j, 2 * j + 1])[None, :]

    def trig(n):
        ang = ((n[:, None] * k) % l).astype(F32) * (2.0 * math.pi / l)
        return jnp.cos(ang), jnp.sin(ang)

    ch, sh = trig(jnp.arange(l // 2 // r, dtype=jnp.int32) * r)
    cl, sl = trig(jnp.arange(r, dtype=jnp.int32))
    scale = l ** -0.5
    cos = (ch[:, None, :] * cl[None, :, :] - sh[:, None, :] * sl[None, :, :]).reshape(l // 2, l)
    sin = (sh[:, None, :] * cl[None, :, :] + ch[:, None, :] * sl[None, :, :]).reshape(l // 2, l)
    return (cos * scale).astype(BF16), (sin * scale).astype(BF16)


def _channel_dft_tables():
    n = jnp.arange(256, dtype=jnp.int32)
    same = (n[:, None] // HEAD_DIM) == (n[None, :] // HEAD_DIM)
    ang = (((n[:, None] % HEAD_DIM) * (n[None, :] % HEAD_DIM)) % HEAD_DIM).astype(F32) * (2.0 * math.pi / HEAD_DIM)
    scale = HEAD_DIM ** -0.5
    cc = jnp.where(same, jnp.cos(ang) * scale, 0.0).astype(BF16)
    sc = jnp.where(same, jnp.sin(ang) * scale, 0.0).astype(BF16)
    return cc, sc


def _group_matrix(width):
    n = jnp.arange(width, dtype=jnp.int32)
    return ((n[:, None] // HEAD_DIM) == (n[None, :] // HEAD_DIM)).astype(BF16)


def _rope_tables(n_tokens):
    t = jnp.arange(n_tokens, dtype=jnp.int32)
    row = (t // GRID_W).astype(F32)
    col = (t % GRID_W).astype(F32)
    n_freq = HEAD_DIM // 4
    inv_freq = ROPE_THETA ** (-jnp.arange(n_freq, dtype=F32) / n_freq)
    ang = jnp.concatenate([row[:, None] * inv_freq, col[:, None] * inv_freq], axis=-1)
    cos, sin = jnp.cos(ang), jnp.sin(ang)
    cos_t = jnp.concatenate([cos, cos, cos, cos], axis=-1)
    sin_t = jnp.concatenate([-sin, sin, -sin, sin], axis=-1)
    return cos_t, sin_t


def _gla_tables():
    c = GLA_CHUNK
    i = jnp.arange(c, dtype=jnp.int32)[:, None]
    t = jnp.arange(c, dtype=jnp.int32)[None, :]
    ones_row = jnp.ones((16, c), jnp.bool_)

    def tables(forward):
        z = [t <= i, t > i] if forward else [t >= i, t < i]
        m = []
        for s in GLA_LEVELS:
            pos = i % (2 * s)
            base = i - pos
            bound = base + s - 1
            right = pos >= s
            if forward:
                zq = right & (t > bound) & (t <= i)
                zk = (~right) & (t > i) & (t <= bound)
            else:
                zq = (~right) & (t >= i) & (t <= bound)
                zk = right & (t > bound) & (t < i)
            z.append(zq | zk)
            same_parent = (i // (2 * s)) == (t // (2 * s))
            t_right = (t % (2 * s)) >= s
            if forward:
                m.append(same_parent & right & (~t_right))
            else:
                m.append(same_parent & (~right) & t_right)
        m.append(i == t)
        z.append(ones_row)
        zt = jnp.concatenate(z, axis=0).astype(BF16)
        m.append(jnp.zeros_like(m[0]))
        mt = jnp.stack([jnp.tile(jnp.concatenate([m[2 * p], m[2 * p + 1]], axis=1), (4, 1))
                        for p in range(len(m) // 2)]).astype(F32)
        return zt, mt

    zf, mf = tables(True)
    zb, mb = tables(False)
    return jnp.stack([zf, zb]), jnp.stack([mf, mb])


def _na_bias_table(rpb):
    c = jnp.arange(GRID_W)[:, None, None]
    kc = jnp.arange(GRID_W)[None, :, None]
    m = jnp.arange(2 * NA_COLS - 1)[None, None, :]
    cs = jnp.clip(c - NA_COLS // 2, 0, GRID_W - NA_COLS)
    valid = (kc >= cs) & (kc < cs + NA_COLS)
    pick = (valid & (kc - c + NA_COLS - 1 == m)).astype(F32)
    by_row = jnp.einsum('hrm,ckm->rhck', rpb.astype(F32), pick, precision=lax.Precision.HIGHEST)
    by_row = jnp.where(valid[None, None, :, :, 0], by_row, NEG)
    blocks = [jnp.concatenate([by_row[j - off + NA_ROWS - 1] for j in range(NA_ROWS)], axis=-1)
              for off in range(NA_ROWS)]
    return jnp.stack(blocks).reshape(NA_ROWS, 4 * GRID_W, NA_ROWS * GRID_W)


_GQA_PERM = (0, 2, 1, 3)


def _layer_weights(l, w_in, na_q_norm, na_k_norm, gla_w_gate_f, gla_b_gate_f, gla_w_gate_b, gla_b_gate_b,
                   gla_norm, gqa_q_norm, gqa_k_norm, w_out):
    wi = w_in[l]
    d = wi.shape[0]
    gq = wi[:, 1824:2080].reshape(d, 4, HEAD_DIM)[:, _GQA_PERM, :].reshape(d, 256)
    w_in_r = jnp.concatenate([wi[:, 0:1792], gq, wi[:, 2080:2336], wi[:, 1792:1824],
                              jnp.zeros((d, 96), F32)], axis=1).astype(BF16)
    wg = jnp.zeros((128, 256), F32)
    wg = wg.at[0:16, 0:128].set(gla_w_gate_f[l]).at[16:32, 128:256].set(gla_w_gate_b[l]).astype(BF16)
    bg = jnp.concatenate([gla_b_gate_f[l], gla_b_gate_b[l]])[None, :]
    wo = w_out[l]
    wo_d = wo[768:1024].reshape(4, HEAD_DIM, -1)[_GQA_PERM, :, :].reshape(256, -1)
    w_out_r = jnp.concatenate([wo[0:768], wo_d], axis=0).astype(BF16)

    def tile4(g):
        return jnp.tile(g, 4)[None, :]

    return {
        "w_in": w_in_r, "wg": wg, "bg": bg, "w_out": w_out_r,
        "naq_g": tile4(na_q_norm[l]), "nak_g": tile4(na_k_norm[l]),
        "gq_g": tile4(gqa_q_norm[l]), "gk_g": tile4(gqa_k_norm[l]),
        "gn": tile4(gla_norm[l]),
    }


def kernel(x, c, ctx, c_ctx, w_mod, b_mod, ffn1_w1, ffn1_w3, ffn1_w2, w_in, na_q_norm, na_k_norm, na_rpb,
           gla_w_gate_f, gla_b_gate_f, gla_w_gate_b, gla_b_gate_b, gla_norm, gqa_q_norm, gqa_k_norm, w_out,
           ffn2_w1, ffn2_w3, ffn2_w2):
    nb, s, d = x.shape
    lc = ctx.shape[1]
    depth = w_mod.shape[0]
    tm_x = TILES["rows_latent"]
    tm_z = TILES["rows_context"]

    cc = jnp.zeros((16, d), F32).at[0:nb].set(c).at[nb].set(c_ctx)
    mod_all = _modulation(cc, w_mod, b_mod)

    consts = {"g256": _group_matrix(256), "g128": _group_matrix(128)}
    consts["cc"], consts["sc"] = _channel_dft_tables()
    cos_t, sin_t = _rope_tables(s)
    dft_x = _dft_tables(s)
    dft_z = _dft_tables(lc)
    z_tab, gla_masks = _gla_tables()
    s_zero = jnp.zeros((nb, 2, 256, 128), F32)

    z = ctx.reshape(1, nb * lc, d)
    four_x = (lambda b, t: (0, t, b), (2, s // 2, nb * 256))
    four_z = (lambda b, t: (0, 0, t), (2, lc // 2, nb * 256))

    def mix_x(b, t):
        return (t, b)

    def mix_z(b, t):
        return (0, t)

    for l in range(depth):
        last = l == depth - 1
        mod = mod_all[l, 0:nb + 1].reshape(nb + 1, N_MOD, d)
        mod_x, mod_z = mod[0:nb], mod[nb:nb + 1]
        lw = _layer_weights(l, w_in, na_q_norm, na_k_norm, gla_w_gate_f, gla_b_gate_f, gla_w_gate_b,
                            gla_b_gate_b, gla_norm, gqa_q_norm, gqa_k_norm, w_out)
        f1 = (ffn1_w1[l].astype(BF16), ffn1_w3[l].astype(BF16), ffn1_w2[l].astype(BF16))
        f2 = (ffn2_w1, ffn2_w3, ffn2_w2)
        tb = _na_bias_table(na_rpb[l])

        px = _ffn_inproj(x, mod_x, f1, lw, consts, cos_t, sin_t, True, tm_x, four_x)
        pz = _ffn_inproj(z, mod_z, f1, lw, consts, cos_t, sin_t, False, tm_z, four_z)
        (x, ucx, usx, naq_x, nak_x, nav_x, qka_x, gv_x, gr_x, q_x, k_x, v_x) = px
        z = pz[0]
        (ucz, usz, naq_z, nak_z, nav_z, qka_z, gv_z, gr_z, q_z, k_z, v_z) = [
            t if t.shape[0] == 2 else t.reshape(nb, lc, t.shape[-1]) for t in pz[1:]]

        a_x = _dft(dft_x[0], dft_x[1], ucx, usx, tm=TILES["dft_rows"])
        b_x = _na(naq_x, nak_x, nav_x, nak_z, nav_z, tb)
        ofz, obz, s_ctx = _gla(qka_z, gv_z, s_zero, z_tab, gla_masks, chunks=lc // GLA_CHUNK)
        ofx, obx, _ = _gla(qka_x, gv_x, s_ctx, z_tab, gla_masks, chunks=TILES["gla_chunks"])
        d_x = _gqa(q_x, k_x, v_x, k_z, v_z, tq=TILES["gqa_queries"], kb=TILES["gqa_keys"])

        x = _outproj(x, mod_x, a_x, mix_x, b_x, ofx, obx, gr_x, d_x, lw["w_out"], consts["g256"],
                     lw["gn"], tm_x)
        x = _ffn(x, mod_x, f2, l, row0=6, tm=tm_x)

        if not last:
            a_z = _dft(dft_z[0], dft_z[1], ucz, usz, tm=lc // 2)
            b_z, d_z = _ctx_attn(naq_z, nak_z, nav_z, q_z, k_z, v_z)
            z = _outproj(z, mod_z, a_z, mix_z, b_z.reshape(1, nb * lc, 256),
                         ofz.reshape(1, nb * lc, 256), obz.reshape(1, nb * lc, 256),
                         gr_z.reshape(1, nb * lc, 256), d_z.reshape(1, nb * lc, 256),
                         lw["w_out"], consts["g256"], lw["gn"], tm_z)
            z = _ffn(z, mod_z, f2, l, row0=6, tm=tm_z)
    return x
```

```python
import functools
import math

import jax
import jax.numpy as jnp
from jax import lax
from jax.experimental import pallas as pl
from jax.experimental.pallas import tpu as pltpu

F32 = jnp.float32
BF16 = jnp.bfloat16

HEAD_DIM = 64
GRID_W = 64
NA_ROWS = 8
NA_COLS = 16
GLA_DK = 32
GLA_TAU = 16.0
GLA_CHUNK = 64
GLA_LEVELS = (32, 16, 8, 4, 2, 1)
ROPE_THETA = 10000.0
EPS = 1e-6
N_MOD = 9
NEG = -1e30

VMEM_LIMIT = 56 * 1024 * 1024

TILES = {
    "rows_latent": 512,
    "rows_context": 256,
    "rows_outproj": 1024,
    "ffn_chunk": 512,
    "mod_cols": 1024,
    "dft_rows": 256,
    "na_rows": 16,
    "gqa_queries": 512,
    "gqa_keys": 2048,
    "softmax_rows": 16,
    "gla_chunks": 8,
}


def _cparams(*sem):
    return pltpu.CompilerParams(dimension_semantics=sem, vmem_limit_bytes=VMEM_LIMIT)


def _dot(a, b):
    return jnp.dot(a, b, preferred_element_type=F32)


def _dot_nt(a, b):
    return lax.dot_general(a, b, (((1,), (1,)), ((), ())), preferred_element_type=F32)


def _dot_tn(a, b):
    return lax.dot_general(a, b, (((0,), (0,)), ((), ())), preferred_element_type=F32)


def _sigmoid(x):
    return 1.0 / (1.0 + jnp.exp(-x))


def _silu(x):
    return x * _sigmoid(x)


def _log_sigmoid(x):
    return jnp.minimum(x, 0.0) - jnp.log(1.0 + jnp.exp(-jnp.abs(x)))


def _const_spec(shape):
    nd = len(shape)
    return pl.BlockSpec(shape, lambda *_: (0,) * nd, pipeline_mode=pl.Buffered(1))


def _mod_kernel(c_ref, w_ref, b_ref, o_ref):
    h = _silu(c_ref[...]).astype(BF16)
    o_ref[0] = _dot(h, w_ref[0].astype(BF16)) + b_ref[0]


def _modulation(cc, w_mod, b_mod):
    depth, d, n = w_mod.shape
    rows = cc.shape[0]
    tn = TILES["mod_cols"]
    return pl.pallas_call(
        _mod_kernel,
        out_shape=jax.ShapeDtypeStruct((depth, rows, n), F32),
        grid=(depth, n // tn),
        in_specs=[
            pl.BlockSpec((rows, d), lambda l, j: (0, 0)),
            pl.BlockSpec((1, d, tn), lambda l, j: (l, 0, j)),
            pl.BlockSpec((1, 1, tn), lambda l, j: (l, 0, j)),
        ],
        out_specs=pl.BlockSpec((1, rows, tn), lambda l, j: (l, 0, j)),
        compiler_params=_cparams("parallel", "parallel"),
        name="adaln_mod",
    )(cc, w_mod, b_mod.reshape(depth, 1, n))


def _norm_mod(x, shift, scale):
    h = x * lax.rsqrt(jnp.mean(x * x, axis=-1, keepdims=True) + EPS)
    return h * (1.0 + scale) + shift


def _swiglu_acc(hb, w1_ref, w3_ref, w2_ref, fc):
    d_ff = w1_ref.shape[1]
    acc = jnp.zeros((hb.shape[0], w2_ref.shape[1]), F32)
    for f0 in range(0, d_ff, fc):
        a = _dot(hb, w1_ref[:, f0:f0 + fc].astype(BF16))
        b = _dot(hb, w3_ref[:, f0:f0 + fc].astype(BF16))
        g = (_silu(a) * b).astype(BF16)
        acc = acc + _dot(g, w2_ref[f0:f0 + fc, :].astype(BF16))
    return acc


def _group_sum(xsq, gmat):
    hi = xsq.astype(BF16)
    lo = (xsq - hi.astype(F32)).astype(BF16)
    return _dot(hi, gmat) + _dot(lo, gmat)


def _head_rms(x, gmat, gain):
    ms = _group_sum(x * x, gmat) * (1.0 / HEAD_DIM)
    return x * lax.rsqrt(ms + EPS) * gain


def _rope128(x, cos, sin_signed):
    lane = lax.broadcasted_iota(jnp.int32, x.shape, 1)
    first_half = (lane % HEAD_DIM) < (HEAD_DIM // 2)
    partner = jnp.where(first_half, pltpu.roll(x, 96, 1), pltpu.roll(x, 32, 1))
    return x * cos + partner * sin_signed


def _inproj_kernel(x_ref, mod_ref, w1_ref, w3_ref, w2_ref, w_ref, g256_ref, g128_ref, cc_ref, sc_ref,
                   naq_g_ref, nak_g_ref, gq_g_ref, gk_g_ref, wg_ref, bg_ref, cos_ref, sin_ref,
                   x1_ref, uc_ref, us_ref, naq_ref, nak_ref, nav_ref, qka_ref, gv_ref, gr_ref,
                   q_ref, k_ref, v_ref, four_ref, *, rope, fc):
    x0 = x_ref[0]
    hb0 = _norm_mod(x0, mod_ref[0, 0:1, :], mod_ref[0, 1:2, :]).astype(BF16)
    x = x0 + (0.5 * mod_ref[0, 2:3, :]) * _swiglu_acc(hb0, w1_ref, w3_ref, w2_ref, fc)
    x1_ref[0] = x
    shift = mod_ref[0, 3:4, :]
    scale = mod_ref[0, 4:5, :]
    hb = _norm_mod(x, shift, scale).astype(BF16)
    h = _dot(hb, w_ref[...])
    g256 = g256_ref[...]
    g128 = g128_ref[...]
    ub = h[:, 0:256].astype(BF16)
    half = four_ref.shape[1] // 2
    for w, (out_ref, tab_ref) in enumerate(((uc_ref, cc_ref), (us_ref, sc_ref))):
        u = _dot(ub, tab_ref[...])
        for j in range(2):
            s = 2 * w + j
            four_ref[s] = u[:, j * 128:(j + 1) * 128]
            out_ref[0, :, j * 128:(j + 1) * 128] = four_ref[s, pl.ds(0, half, stride=2), :].astype(BF16)
            out_ref[1, :, j * 128:(j + 1) * 128] = four_ref[s, pl.ds(1, half, stride=2), :].astype(BF16)
    naq_ref[0] = (_head_rms(h[:, 256:512], g256, naq_g_ref[...]) * HEAD_DIM ** -0.5).astype(BF16)
    nak_ref[0] = _head_rms(h[:, 512:768], g256, nak_g_ref[...]).astype(BF16)
    nav_ref[0] = h[:, 768:1024].astype(BF16)
    gates = _dot(h[:, 2304:2432].astype(BF16), wg_ref[...]) + bg_ref[...]
    log_a = _log_sigmoid(gates) / GLA_TAU
    qka_ref[0, :, 0:128] = h[:, 1024:1152] * GLA_DK ** -0.5
    qka_ref[0, :, 128:256] = h[:, 1152:1280]
    qka_ref[0, :, 256:512] = log_a
    gv_ref[0] = h[:, 1280:1536].astype(BF16)
    gr_ref[0] = _silu(h[:, 1536:1792]).astype(BF16)
    q = _head_rms(h[:, 1792:2048], g256, gq_g_ref[...])
    k = _head_rms(h[:, 2048:2176], g128, gk_g_ref[:, 0:128])
    if rope:
        cos = cos_ref[...]
        sin = sin_ref[...]
        q = jnp.concatenate([_rope128(q[:, 0:128], cos, sin), _rope128(q[:, 128:256], cos, sin)], axis=1)
        k = _rope128(k, cos, sin)
    q_ref[0] = (q * HEAD_DIM ** -0.5).astype(BF16)
    k_ref[0] = k.astype(BF16)
    v_ref[0] = h[:, 2176:2304].astype(BF16)


def _ffn_inproj(x, mod, ffn_w, lw, consts, cos_t, sin_t, rope, tm, four_index):
    nb, r, d = x.shape
    n_four = four_index[1]
    w1, w3, w2 = ffn_w

    def act(width, dtype):
        return jax.ShapeDtypeStruct((nb, r, width), dtype)

    def act_spec(width):
        return pl.BlockSpec((1, tm, width), lambda b, t: (b, t, 0))

    four_shape = jax.ShapeDtypeStruct(n_four, BF16)
    four_spec = pl.BlockSpec((2, tm // 2, 256), four_index[0])
    vec256 = _const_spec((1, 256))
    out_shapes = [act(d, F32), four_shape, four_shape, act(256, BF16), act(256, BF16), act(256, BF16),
                  act(512, F32), act(256, BF16), act(256, BF16),
                  act(256, BF16), act(128, BF16), act(128, BF16)]
    out_specs = [act_spec(d), four_spec, four_spec, act_spec(256), act_spec(256), act_spec(256),
                 act_spec(512), act_spec(256), act_spec(256),
                 act_spec(256), act_spec(128), act_spec(128)]
    return pl.pallas_call(
        functools.partial(_inproj_kernel, rope=rope, fc=TILES["ffn_chunk"]),
        out_shape=out_shapes,
        grid=(nb, r // tm),
        in_specs=[
            pl.BlockSpec((1, tm, d), lambda b, t: (b, t, 0)),
            pl.BlockSpec((1, N_MOD, d), lambda b, t: (b, 0, 0)),
            _const_spec(w1.shape), _const_spec(w3.shape), _const_spec(w2.shape),
            _const_spec(lw["w_in"].shape),
            _const_spec((256, 256)), _const_spec((128, 128)),
            _const_spec((256, 256)), _const_spec((256, 256)),
            vec256, vec256, vec256, vec256,
            _const_spec((128, 256)), vec256,
            pl.BlockSpec((tm, 128), lambda b, t: (t, 0)),
            pl.BlockSpec((tm, 128), lambda b, t: (t, 0)),
        ],
        out_specs=out_specs,
        scratch_shapes=[pltpu.VMEM((4, tm, 128), F32)],
        compiler_params=_cparams("parallel", "parallel"),
        name="ffn1_in_proj",
    )(x, mod, w1, w3, w2, lw["w_in"], consts["g256"], consts["g128"], consts["cc"], consts["sc"],
      lw["naq_g"], lw["nak_g"], lw["gq_g"], lw["gk_g"], lw["wg"], lw["bg"], cos_t, sin_t)


def _dft_kernel(c_ref, s_ref, uc_ref, us_ref, lo_ref, hi_ref):
    h = uc_ref.shape[1]
    a_e = _dot(c_ref[:, 0:h], uc_ref[0]) - _dot(s_ref[:, 0:h], us_ref[0])
    a_o = _dot(c_ref[:, h:2 * h], uc_ref[1]) - _dot(s_ref[:, h:2 * h], us_ref[1])
    lo_ref[...] = (a_e + a_o).astype(lo_ref.dtype)
    hi_ref[...] = (a_e - a_o).astype(hi_ref.dtype)


def _dft(c_tab, s_tab, uc, us, tm):
    _, h, n = uc.shape
    half = jax.ShapeDtypeStruct((h, n), BF16)
    lo, hi = pl.pallas_call(
        _dft_kernel,
        out_shape=[half, half],
        grid=(h // tm,),
        in_specs=[
            pl.BlockSpec((tm, 2 * h), lambda i: (i, 0)),
            pl.BlockSpec((tm, 2 * h), lambda i: (i, 0)),
            _const_spec(uc.shape),
            _const_spec(us.shape),
        ],
        out_specs=[pl.BlockSpec((tm, n), lambda i: (i, 0)), pl.BlockSpec((tm, n), lambda i: (i, 0))],
        compiler_params=_cparams("parallel"),
        name="position_dft",
    )(c_tab, s_tab, uc, us)
    return jnp.concatenate([lo, hi], axis=0)


def _stack_heads(q, n_heads, width):
    lane = lax.broadcasted_iota(jnp.int32, q.shape, 1)
    zero = jnp.zeros_like(q)
    return jnp.concatenate([jnp.where(lane // width == h, q, zero) for h in range(n_heads)], axis=0)


def _unstack_heads(o, n_heads, width):
    m = o.shape[0] // n_heads
    lane = lax.broadcasted_iota(jnp.int32, (m, o.shape[1]), 1)
    out = jnp.zeros((m, o.shape[1]), F32)
    for h in range(n_heads):
        out = out + jnp.where(lane // width == h, o[h * m:(h + 1) * m, :], 0.0)
    return out


def _softmax_pv(score_blocks, value_blocks):
    m = score_blocks[0].max(axis=-1, keepdims=True)
    for s in score_blocks[1:]:
        m = jnp.maximum(m, s.max(axis=-1, keepdims=True))
    l = None
    o = None
    for s, v in zip(score_blocks, value_blocks):
        p = jnp.exp(s - m)
        ps = p.sum(axis=-1, keepdims=True)
        pv = _dot(p.astype(BF16), v)
        l = ps if l is None else l + ps
        o = pv if o is None else o + pv
    return o * (1.0 / l)


def _na_kernel(q_ref, k_ref, v_ref, kc_ref, vc_ref, tb_ref, o_ref, *, rows_per_step, n_rows):
    rb = pl.program_id(1)
    kc = kc_ref[0]
    vc = vc_ref[0]
    for i in range(rows_per_step):
        r = rb * rows_per_step + i
        rs = jnp.clip(r - NA_ROWS // 2, 0, n_rows - NA_ROWS)
        off = r - rs
        q = q_ref[0, i * GRID_W:(i + 1) * GRID_W, :]
        qs = _stack_heads(q, 4, HEAD_DIM)
        start = pl.multiple_of(rs * GRID_W, GRID_W)
        kw = k_ref[0, pl.ds(start, NA_ROWS * GRID_W), :]
        vw = v_ref[0, pl.ds(start, NA_ROWS * GRID_W), :]
        s_w = _dot_nt(qs, kw) + tb_ref[off]
        s_c = _dot_nt(qs, kc)
        o = _softmax_pv([s_w, s_c], [vw, vc])
        o_ref[0, i * GRID_W:(i + 1) * GRID_W, :] = _unstack_heads(o, 4, HEAD_DIM).astype(BF16)


def _na(q, k, v, kc, vc, tb):
    nb, s, w = q.shape
    n_rows = s // GRID_W
    rps = TILES["na_rows"]
    lc = kc.shape[1]
    return pl.pallas_call(
        functools.partial(_na_kernel, rows_per_step=rps, n_rows=n_rows),
        out_shape=jax.ShapeDtypeStruct((nb, s, w), BF16),
        grid=(nb, n_rows // rps),
        in_specs=[
            pl.BlockSpec((1, rps * GRID_W, w), lambda b, t: (b, t, 0)),
            pl.BlockSpec((1, s, w), lambda b, t: (b, 0, 0)),
            pl.BlockSpec((1, s, w), lambda b, t: (b, 0, 0)),
            pl.BlockSpec((1, lc, w), lambda b, t: (b, 0, 0)),
            pl.BlockSpec((1, lc, w), lambda b, t: (b, 0, 0)),
            _const_spec(tb.shape),
        ],
        out_specs=pl.BlockSpec((1, rps * GRID_W, w), lambda b, t: (b, t, 0)),
        compiler_params=_cparams("parallel", "parallel"),
        name="neighborhood_attn",
    )(q, k, v, kc, vc, tb)


def _gqa_scores_lhs(q):
    return jnp.concatenate([_stack_heads(q[:, 0:128], 2, HEAD_DIM),
                            _stack_heads(q[:, 128:256], 2, HEAD_DIM)], axis=0)


def _gqa_merge(o):
    m = o.shape[0] // 4
    return jnp.concatenate([_unstack_heads(o[0:2 * m], 2, HEAD_DIM),
                            _unstack_heads(o[2 * m:4 * m], 2, HEAD_DIM)], axis=1)


def _gqa_kernel(q_ref, k_ref, v_ref, kc_ref, vc_ref, o_ref, s_ref, p_ref, l_ref, *, kb, rb):
    qs = _gqa_scores_lhs(q_ref[0])
    n = qs.shape[0]
    m = jnp.full((n, 1), NEG, F32)
    acc = jnp.zeros((n, 128), F32)
    l_ref[...] = jnp.zeros_like(l_ref)
    blocks = [(k_ref, v_ref, j * kb, kb) for j in range(k_ref.shape[1] // kb)]
    blocks.append((kc_ref, vc_ref, 0, kc_ref.shape[1]))
    for kr, vr, k0, size in blocks:
        s = _dot_nt(qs, kr[0, k0:k0 + size, :])
        s_ref[:, 0:size] = s
        m_new = jnp.maximum(m, s.max(axis=-1, keepdims=True))
        alpha = jnp.exp(m - m_new)
        for r0 in range(0, n, rb):
            p = jnp.exp(s_ref[r0:r0 + rb, 0:size] - m_new[r0:r0 + rb])
            part = p[:, 0:128]
            for j in range(1, size // 128):
                part = part + p[:, j * 128:(j + 1) * 128]
            l_ref[r0:r0 + rb, :] = l_ref[r0:r0 + rb, :] * alpha[r0:r0 + rb] + part
            p_ref[r0:r0 + rb, 0:size] = p.astype(BF16)
        acc = acc * alpha + _dot(p_ref[:, 0:size], vr[0, k0:k0 + size, :])
        m = m_new
    l = l_ref[...].sum(axis=-1, keepdims=True)
    o_ref[0] = _gqa_merge(acc * (1.0 / l)).astype(BF16)


def _gqa(q, k, v, kc, vc, tq, kb):
    nb, s, _ = q.shape
    lc = kc.shape[1]
    return pl.pallas_call(
        functools.partial(_gqa_kernel, kb=kb, rb=TILES["softmax_rows"]),
        out_shape=jax.ShapeDtypeStruct((nb, s, 256), BF16),
        grid=(nb, s // tq),
        in_specs=[
            pl.BlockSpec((1, tq, 256), lambda b, t: (b, t, 0)),
            pl.BlockSpec((1, s, 128), lambda b, t: (b, 0, 0)),
            pl.BlockSpec((1, s, 128), lambda b, t: (b, 0, 0)),
            pl.BlockSpec((1, lc, 128), lambda b, t: (b, 0, 0)),
            pl.BlockSpec((1, lc, 128), lambda b, t: (b, 0, 0)),
        ],
        out_specs=pl.BlockSpec((1, tq, 256), lambda b, t: (b, t, 0)),
        scratch_shapes=[pltpu.VMEM((4 * tq, kb), F32), pltpu.VMEM((4 * tq, kb), BF16),
                        pltpu.VMEM((4 * tq, 128), F32)],
        compiler_params=_cparams("parallel", "parallel"),
        name="gqa_attn",
    )(q, k, v, kc, vc)


def _ctx_attn_kernel(naq_ref, nak_ref, nav_ref, q_ref, k_ref, v_ref, bz_ref, dz_ref):
    qs = _stack_heads(naq_ref[0], 4, HEAD_DIM)
    o = _softmax_pv([_dot_nt(qs, nak_ref[0])], [nav_ref[0]])
    bz_ref[0] = _unstack_heads(o, 4, HEAD_DIM).astype(BF16)
    qg = _gqa_scores_lhs(q_ref[0])
    og = _softmax_pv([_dot_nt(qg, k_ref[0])], [v_ref[0]])
    dz_ref[0] = _gqa_merge(og).astype(BF16)


def _ctx_attn(naq, nak, nav, q, k, v):
    nb, lc, _ = naq.shape

    def spec(w):
        return pl.BlockSpec((1, lc, w), lambda b: (b, 0, 0))

    out = jax.ShapeDtypeStruct((nb, lc, 256), BF16)
    return pl.pallas_call(
        _ctx_attn_kernel,
        out_shape=[out, out],
        grid=(nb,),
        in_specs=[spec(256), spec(256), spec(256), spec(256), spec(128), spec(128)],
        out_specs=[spec(256), spec(256)],
        compiler_params=_cparams("parallel"),
        name="context_attn",
    )(naq, nak, nav, q, k, v)


def _gla_kernel(qf_ref, vf_ref, qb_ref, vb_ref, z_ref, mask_ref, s0_ref, of_ref, ob_ref, sfin_ref,
                st_ref, *, chunks):
    p = pl.program_id(1)
    c = GLA_CHUNK

    @pl.when(p == 0)
    def _():
        st_ref[...] = s0_ref[0]

    row = lax.broadcasted_iota(jnp.int32, (256, 128), 0)
    lane = lax.broadcasted_iota(jnp.int32, (256, 128), 1)
    bd_mask = (row // HEAD_DIM) == (lane // GLA_DK)
    row4 = lax.broadcasted_iota(jnp.int32, (4 * c, 128), 0)
    lane4 = lax.broadcasted_iota(jnp.int32, (4 * c, 128), 1)
    head_mask = (row4 // c) == (lane4 // GLA_DK)
    n_lev = len(GLA_LEVELS)
    zq = jnp.zeros((4 * c, 128), BF16)
    zk = jnp.zeros((c, 128), BF16)

    units = []
    for ci in range(chunks):
        units.append((0, qf_ref, vf_ref, of_ref, ci * c, 256))
        units.append((1, qb_ref, vb_ref, ob_ref, (chunks - 1 - ci) * c, 384))

    loaded = []
    for d, in_ref, v_ref, o_ref, r0, a0 in units:
        q = in_ref[0, r0:r0 + c, 0:128]
        k = in_ref[0, r0:r0 + c, 128:256]
        a = in_ref[0, r0:r0 + c, a0:a0 + 128]
        v = v_ref[0, r0:r0 + c, :]
        a_hi = a.astype(BF16)
        a_lo = (a - a_hi.astype(F32)).astype(BF16)
        e2 = _dot(z_ref[d], jnp.concatenate([a_hi, a_lo], axis=1))
        x = jnp.exp(e2[:, 0:128] + e2[:, 128:256])
        loaded.append((q, k, v, x))

    st = [st_ref[0], st_ref[1]]
    for (d, _, _, o_ref, r0, _), (q, k, v, x) in zip(units, loaded):

        def level_operands(lev):
            if lev < n_lev:
                xl = x[(2 + lev) * c:(3 + lev) * c]
                ql, kl = q * xl, k * xl
            else:
                ql, kl = q, k
            qs = jnp.where(head_mask, jnp.concatenate([ql, ql, ql, ql], axis=0), 0.0)
            return qs.astype(BF16), kl.astype(BF16)

        scores = jnp.zeros((4 * c, 2 * c), F32)
        for pair in range((n_lev + 2) // 2):
            qa, ka = level_operands(2 * pair)
            qb, kb = level_operands(2 * pair + 1) if 2 * pair + 1 <= n_lev else (zq, zk)
            lhs = jnp.concatenate([qa, qb], axis=1)
            rhs = jnp.concatenate([jnp.concatenate([ka, zk], axis=1),
                                   jnp.concatenate([zk, kb], axis=1)], axis=0)
            scores = scores + _dot_nt(lhs, rhs) * mask_ref[d, pair]
        o_all = _dot(scores.astype(BF16), jnp.concatenate([v, v], axis=0))
        o_intra = _unstack_heads(o_all, 4, HEAD_DIM)
        qe = (q * x[0:c]).astype(BF16)
        ke = (k * x[c:2 * c]).astype(BF16)
        upd = jnp.where(bd_mask, _dot_tn(v, ke), 0.0)
        decay = x[8 * c:8 * c + 1, :]
        o_ref[0, r0:r0 + c, :] = (o_intra + _dot_nt(qe, st[d].astype(BF16))).astype(BF16)
        st[d] = st[d] * decay + upd
    st_ref[0] = st[0]
    st_ref[1] = st[1]

    @pl.when(p == pl.num_programs(1) - 1)
    def _():
        sfin_ref[0] = st_ref[...]


def _gla(qka, v, s0, z_tab, masks, chunks):
    nb, r, _ = qka.shape
    blk = chunks * GLA_CHUNK
    nblk = r // blk
    out = jax.ShapeDtypeStruct((nb, r, 256), BF16)
    return pl.pallas_call(
        functools.partial(_gla_kernel, chunks=chunks),
        out_shape=[out, out, jax.ShapeDtypeStruct((nb, 2, 256, 128), F32)],
        grid=(nb, nblk),
        in_specs=[
            pl.BlockSpec((1, blk, 512), lambda b, p: (b, p, 0)),
            pl.BlockSpec((1, blk, 256), lambda b, p: (b, p, 0)),
            pl.BlockSpec((1, blk, 512), lambda b, p: (b, nblk - 1 - p, 0)),
            pl.BlockSpec((1, blk, 256), lambda b, p: (b, nblk - 1 - p, 0)),
            _const_spec(z_tab.shape),
            _const_spec(masks.shape),
            pl.BlockSpec((1, 2, 256, 128), lambda b, p: (b, 0, 0, 0)),
        ],
        out_specs=[
            pl.BlockSpec((1, blk, 256), lambda b, p: (b, p, 0)),
            pl.BlockSpec((1, blk, 256), lambda b, p: (b, nblk - 1 - p, 0)),
            pl.BlockSpec((1, 2, 256, 128), lambda b, p: (b, 0, 0, 0)),
        ],
        scratch_shapes=[pltpu.VMEM((2, 256, 128), F32)],
        compiler_params=_cparams("parallel", "arbitrary"),
        name="gla_scan",
    )(qka, v, qka, v, z_tab, masks, s0)


def _outproj_kernel(x_ref, mod_ref, a_ref, b_ref, of_ref, ob_ref, r_ref, d_ref, w_ref, g256_ref,
                    gn_ref, o_ref):
    o = of_ref[0].astype(F32) + ob_ref[0].astype(F32)
    cx = (_head_rms(o, g256_ref[...], gn_ref[...]) * r_ref[0].astype(F32)).astype(BF16)
    acc = _dot(a_ref[...], w_ref[0:256, :])
    acc = acc + _dot(b_ref[0], w_ref[256:512, :])
    acc = acc + _dot(cx, w_ref[512:768, :])
    acc = acc + _dot(d_ref[0], w_ref[768:1024, :])
    o_ref[0] = x_ref[0] + mod_ref[0, 5:6, :] * acc


def _outproj(x, mod, a, a_index, b, o_f, o_b, r, dd, w_out, g256, gn, tm):
    nb, rows, d = x.shape

    def act_spec(width):
        return pl.BlockSpec((1, tm, width), lambda bb, t: (bb, t, 0))

    return pl.pallas_call(
        _outproj_kernel,
        out_shape=jax.ShapeDtypeStruct(x.shape, F32),
        grid=(nb, rows // tm),
        in_specs=[
            act_spec(d),
            pl.BlockSpec((1, N_MOD, d), lambda bb, t: (bb, 0, 0)),
            pl.BlockSpec((tm, 256), a_index),
            act_spec(256), act_spec(256), act_spec(256), act_spec(256), act_spec(256),
            _const_spec(w_out.shape), _const_spec((256, 256)), _const_spec((1, 256)),
        ],
        out_specs=act_spec(d),
        compiler_params=_cparams("parallel", "parallel"),
        name="out_proj",
    )(x, mod, a, b, o_f, o_b, r, dd, w_out, g256, gn)


def _ffn_kernel(x_ref, mod_ref, w1_ref, w3_ref, w2_ref, o_ref, *, row0, fc):
    x = x_ref[0]
    hb = _norm_mod(x, mod_ref[0, row0:row0 + 1, :], mod_ref[0, row0 + 1:row0 + 2, :]).astype(BF16)
    o_ref[0] = x + (0.5 * mod_ref[0, row0 + 2:row0 + 3, :]) * _swiglu_acc(hb, w1_ref, w3_ref, w2_ref, fc)


def _ffn(x, mod, ffn_w, layer, row0, tm):
    nb, r, d = x.shape
    w1, w3, w2 = ffn_w

    def layer_spec(w):
        return pl.BlockSpec((None,) + w.shape[1:], lambda b, t: (layer, 0, 0), pipeline_mode=pl.Buffered(1))

    return pl.pallas_call(
        functools.partial(_ffn_kernel, row0=row0, fc=TILES["ffn_chunk"]),
        out_shape=jax.ShapeDtypeStruct(x.shape, F32),
        grid=(nb, r // tm),
        in_specs=[
            pl.BlockSpec((1, tm, d), lambda b, t: (b, t, 0)),
            pl.BlockSpec((1, N_MOD, d), lambda b, t: (b, 0, 0)),
            layer_spec(w1), layer_spec(w3), layer_spec(w2),
        ],
        out_specs=pl.BlockSpec((1, tm, d), lambda b, t: (b, t, 0)),
        compiler_params=_cparams("parallel", "parallel"),
        name="ffn_halfstep",
    )(x, mod, w1, w3, w2)


def _dft_tables(l):
    r = 1 << (int(math.log2(l)) // 2)
    j = jnp.arange(l // 2, dtype=jnp.int32)
    k = jnp.concatenate([2 * j, 2 * j + 1])[None, :]

    def trig(n):
        ang = ((n[:, None] * k) % l).astype(F32) * (2.0 * math.pi / l)
        return jnp.cos(ang), jnp.sin(ang)

    ch, sh = trig(jnp.arange(l // 2 // r, dtype=jnp.int32) * r)
    cl, sl = trig(jnp.arange(r, dtype=jnp.int32))
    scale = l ** -0.5
    cos = (ch[:, None, :] * cl[None, :, :] - sh[:, None, :] * sl[None, :, :]).reshape(l // 2, l)
    sin = (sh[:, None, :] * cl[None, :, :] + ch[:, None, :] * sl[None, :, :]).reshape(l // 2, l)
    return (cos * scale).astype(BF16), (sin * scale).astype(BF16)


def _channel_dft_tables():
    n = jnp.arange(256, dtype=jnp.int32)
    same = (n[:, None] // HEAD_DIM) == (n[None, :] // HEAD_DIM)
    ang = (((n[:, None] % HEAD_DIM) * (n[None, :] % HEAD_DIM)) % HEAD_DIM).astype(F32) * (2.0 * math.pi / HEAD_DIM)
    scale = HEAD_DIM ** -0.5
    cc = jnp.where(same, jnp.cos(ang) * scale, 0.0).astype(BF16)
    sc = jnp.where(same, jnp.sin(ang) * scale, 0.0).astype(BF16)
    return cc, sc


def _group_matrix(width):
    n = jnp.arange(width, dtype=jnp.int32)
    return ((n[:, None] // HEAD_DIM) == (n[None, :] // HEAD_DIM)).astype(BF16)


def _rope_tables(n_tokens):
    t = jnp.arange(n_tokens, dtype=jnp.int32)
    row = (t // GRID_W).astype(F32)
    col = (t % GRID_W).astype(F32)
    n_freq = HEAD_DIM // 4
    inv_freq = ROPE_THETA ** (-jnp.arange(n_freq, dtype=F32) / n_freq)
    ang = jnp.concatenate([row[:, None] * inv_freq, col[:, None] * inv_freq], axis=-1)
    cos, sin = jnp.cos(ang), jnp.sin(ang)
    cos_t = jnp.concatenate([cos, cos, cos, cos], axis=-1)
    sin_t = jnp.concatenate([-sin, sin, -sin, sin], axis=-1)
    return cos_t, sin_t


def _gla_tables():
    c = GLA_CHUNK
    i = jnp.arange(c, dtype=jnp.int32)[:, None]
    t = jnp.arange(c, dtype=jnp.int32)[None, :]
    ones_row = jnp.ones((16, c), jnp.bool_)

    def tables(forward):
        z = [t <= i, t > i] if forward else [t >= i, t < i]
        m = []
        for s in GLA_LEVELS:
            pos = i % (2 * s)
            base = i - pos
            bound = base + s - 1
            right = pos >= s
            if forward:
                zq = right & (t > bound) & (t <= i)
                zk = (~right) & (t > i) & (t <= bound)
            else:
                zq = (~right) & (t >= i) & (t <= bound)
                zk = right & (t > bound) & (t < i)
            z.append(zq | zk)
            same_parent = (i // (2 * s)) == (t // (2 * s))
            t_right = (t % (2 * s)) >= s
            if forward:
                m.append(same_parent & right & (~t_right))
            else:
                m.append(same_parent & (~right) & t_right)
        m.append(i == t)
        z.append(ones_row)
        zt = jnp.concatenate(z, axis=0).astype(BF16)
        m.append(jnp.zeros_like(m[0]))
        mt = jnp.stack([jnp.tile(jnp.concatenate([m[2 * p], m[2 * p + 1]], axis=1), (4, 1))
                        for p in range(len(m) // 2)]).astype(F32)
        return zt, mt

    zf, mf = tables(True)
    zb, mb = tables(False)
    return jnp.stack([zf, zb]), jnp.stack([mf, mb])


def _na_bias_table(rpb):
    c = jnp.arange(GRID_W)[:, None, None]
    kc = jnp.arange(GRID_W)[None, :, None]
    m = jnp.arange(2 * NA_COLS - 1)[None, None, :]
    cs = jnp.clip(c - NA_COLS // 2, 0, GRID_W - NA_COLS)
    valid = (kc >= cs) & (kc < cs + NA_COLS)
    pick = (valid & (kc - c + NA_COLS - 1 == m)).astype(F32)
    by_row = jnp.einsum('hrm,ckm->rhck', rpb.astype(F32), pick, precision=lax.Precision.HIGHEST)
    by_row = jnp.where(valid[None, None, :, :, 0], by_row, NEG)
    blocks = [jnp.concatenate([by_row[j - off + NA_ROWS - 1] for j in range(NA_ROWS)], axis=-1)
              for off in range(NA_ROWS)]
    return jnp.stack(blocks).reshape(NA_ROWS, 4 * GRID_W, NA_ROWS * GRID_W)


_GQA_PERM = (0, 2, 1, 3)


def _layer_weights(l, w_in, na_q_norm, na_k_norm, gla_w_gate_f, gla_b_gate_f, gla_w_gate_b, gla_b_gate_b,
                   gla_norm, gqa_q_norm, gqa_k_norm, w_out):
    wi = w_in[l]
    d = wi.shape[0]
    gq = wi[:, 1824:2080].reshape(d, 4, HEAD_DIM)[:, _GQA_PERM, :].reshape(d, 256)
    w_in_r = jnp.concatenate([wi[:, 0:1792], gq, wi[:, 2080:2336], wi[:, 1792:1824],
                              jnp.zeros((d, 96), F32)], axis=1).astype(BF16)
    wg = jnp.zeros((128, 256), F32)
    wg = wg.at[0:16, 0:128].set(gla_w_gate_f[l]).at[16:32, 128:256].set(gla_w_gate_b[l]).astype(BF16)
    bg = jnp.concatenate([gla_b_gate_f[l], gla_b_gate_b[l]])[None, :]
    wo = w_out[l]
    wo_d = wo[768:1024].reshape(4, HEAD_DIM, -1)[_GQA_PERM, :, :].reshape(256, -1)
    w_out_r = jnp.concatenate([wo[0:768], wo_d], axis=0).astype(BF16)

    def tile4(g):
        return jnp.tile(g, 4)[None, :]

    return {
        "w_in": w_in_r, "wg": wg, "bg": bg, "w_out": w_out_r,
        "naq_g": tile4(na_q_norm[l]), "nak_g": tile4(na_k_norm[l]),
        "gq_g": tile4(gqa_q_norm[l]), "gk_g": tile4(gqa_k_norm[l]),
        "gn": tile4(gla_norm[l]),
    }


def kernel(x, c, ctx, c_ctx, w_mod, b_mod, ffn1_w1, ffn1_w3, ffn1_w2, w_in, na_q_norm, na_k_norm, na_rpb,
           gla_w_gate_f, gla_b_gate_f, gla_w_gate_b, gla_b_gate_b, gla_norm, gqa_q_norm, gqa_k_norm, w_out,
           ffn2_w1, ffn2_w3, ffn2_w2):
    nb, s, d = x.shape
    lc = ctx.shape[1]
    depth = w_mod.shape[0]
    tm_x = TILES["rows_latent"]
    tm_z = TILES["rows_context"]

    cc = jnp.zeros((16, d), F32).at[0:nb].set(c).at[nb].set(c_ctx)
    mod_all = _modulation(cc, w_mod, b_mod)

    consts = {"g256": _group_matrix(256), "g128": _group_matrix(128)}
    consts["cc"], consts["sc"] = _channel_dft_tables()
    cos_t, sin_t = _rope_tables(s)
    dft_x = _dft_tables(s)
    dft_z = _dft_tables(lc)
    z_tab, gla_masks = _gla_tables()
    s_zero = jnp.zeros((nb, 2, 256, 128), F32)

    z = ctx.reshape(1, nb * lc, d)
    four_x = (lambda b, t: (0, t, b), (2, s // 2, nb * 256))
    four_z = (lambda b, t: (0, 0, t), (2, lc // 2, nb * 256))

    def mix_x(b, t):
        return (t, b)

    def mix_z(b, t):
        return (0, t)

    for l in range(depth):
        last = l == depth - 1
        mod = mod_all[l, 0:nb + 1].reshape(nb + 1, N_MOD, d)
        mod_x, mod_z = mod[0:nb], mod[nb:nb + 1]
        lw = _layer_weights(l, w_in, na_q_norm, na_k_norm, gla_w_gate_f, gla_b_gate_f, gla_w_gate_b,
                            gla_b_gate_b, gla_norm, gqa_q_norm, gqa_k_norm, w_out)
        f1 = (ffn1_w1[l].astype(BF16), ffn1_w3[l].astype(BF16), ffn1_w2[l].astype(BF16))
        f2 = (ffn2_w1, ffn2_w3, ffn2_w2)
        tb = _na_bias_table(na_rpb[l])

        px = _ffn_inproj(x, mod_x, f1, lw, consts, cos_t, sin_t, True, tm_x, four_x)
        pz = _ffn_inproj(z, mod_z, f1, lw, consts, cos_t, sin_t, False, tm_z, four_z)
        (x, ucx, usx, naq_x, nak_x, nav_x, qka_x, gv_x, gr_x, q_x, k_x, v_x) = px
        z = pz[0]
        (ucz, usz, naq_z, nak_z, nav_z, qka_z, gv_z, gr_z, q_z, k_z, v_z) = [
            t if t.shape[0] == 2 else t.reshape(nb, lc, t.shape[-1]) for t in pz[1:]]

        a_x = _dft(dft_x[0], dft_x[1], ucx, usx, tm=TILES["dft_rows"])
        b_x = _na(naq_x, nak_x, nav_x, nak_z, nav_z, tb)
        ofz, obz, s_ctx = _gla(qka_z, gv_z, s_zero, z_tab, gla_masks, chunks=lc // GLA_CHUNK)
        ofx, obx, _ = _gla(qka_x, gv_x, s_ctx, z_tab, gla_masks, chunks=TILES["gla_chunks"])
        d_x = _gqa(q_x, k_x, v_x, k_z, v_z, tq=TILES["gqa_queries"], kb=TILES["gqa_keys"])

        x = _outproj(x, mod_x, a_x, mix_x, b_x, ofx, obx, gr_x, d_x, lw["w_out"], consts["g256"],
                     lw["gn"], TILES["rows_outproj"])
        x = _ffn(x, mod_x, f2, l, row0=6, tm=tm_x)

        if not last:
            a_z = _dft(dft_z[0], dft_z[1], ucz, usz, tm=lc // 2)
            b_z, d_z = _ctx_attn(naq_z, nak_z, nav_z, q_z, k_z, v_z)
            z = _outproj(z, mod_z, a_z, mix_z, b_z.reshape(1, nb * lc, 256),
                         ofz.reshape(1, nb * lc, 256), obz.reshape(1, nb * lc, 256),
                         gr_z.reshape(1, nb * lc, 256), d_z.reshape(1, nb * lc, 256),
                         lw["w_out"], consts["g256"], lw["gn"], tm_z)
            z = _ffn(z, mod_z, f2, l, row0=6, tm=tm_z)
    return x
```
